```python
import math
import jax, jax.numpy as jnp
from jax import lax
import numpy as np


D_MODEL = 2048
BATCH = 1
SEQ = 8192
DEPTH = 2
DEC_BATCH = 8
DEC_SEQ = 2048
PAST_LEN = 128

D_MIX = D_MODEL
D_ATTN = D_MIX // 2
D_HYENA = D_MIX - D_ATTN
HEAD_DIM = 64
N_HEADS = D_ATTN // HEAD_DIM
N_KV_HEADS = N_HEADS // 4
GQA_GROUPS = N_HEADS // N_KV_HEADS
ROT_DIM = HEAD_DIM // 4
ROPE_THETA = 500000.0
WINDOW = 128
BLOCK = 128
HYENA_ORDER = 2
FILTER_BANDS = 16
FILTER_EMB = 2 * FILTER_BANDS + 1
FILTER_HIDDEN = 64
N_FILTERS = HYENA_ORDER * 2 * D_HYENA
DECAY_SLOW = -math.log(1e-2) / 1.5
DECAY_FAST = -math.log(1e-2) / 0.3
DN_ALPHA = (2.0 * DEPTH) ** 0.25
DN_BETA = (8.0 * DEPTH) ** -0.25
NORM_EPS = 1e-5
MASK_VALUE = -1e30
Q_COLS = N_HEADS * HEAD_DIM
KV_COLS = N_KV_HEADS * HEAD_DIM
SECTION_SIZES = (Q_COLS, KV_COLS, KV_COLS, D_ATTN, 3 * D_HYENA, D_HYENA)
SPLIT_POINTS = tuple(sum(SECTION_SIZES[:i + 1]) for i in range(len(SECTION_SIZES) - 1))
D_IN = sum(SECTION_SIZES)
V_START = Q_COLS + KV_COLS
HYV_START = Q_COLS + 2 * KV_COLS + D_ATTN

kernel_name = 'hymba_swa_hyena_bidir_encoder'


def _layernorm(t, g, b):
    tf = t.astype(jnp.float32)
    mu = jnp.mean(tf, axis=-1, keepdims=True)
    var = jnp.mean(jnp.square(tf - mu), axis=-1, keepdims=True)
    return ((tf - mu) * lax.rsqrt(var + NORM_EPS) * g + b).astype(t.dtype)


def _rmsnorm(t, g):
    tf = t.astype(jnp.float32)
    return (tf * lax.rsqrt(jnp.mean(jnp.square(tf), axis=-1, keepdims=True) + NORM_EPS) * g).astype(t.dtype)


def _partial_rope(t):
    L = t.shape[1]
    inv = ROPE_THETA ** (-jnp.arange(0, ROT_DIM, 2, dtype=jnp.float32) / ROT_DIM)
    ang = jnp.arange(L, dtype=jnp.float32)[:, None] * inv[None, :]
    cos = jnp.cos(ang)[None, :, None, :]
    sin = jnp.sin(ang)[None, :, None, :]
    tr = t[..., :ROT_DIM].astype(jnp.float32)
    t1, t2 = tr[..., :ROT_DIM // 2], tr[..., ROT_DIM // 2:]
    rot = jnp.concatenate([t1 * cos - t2 * sin, t2 * cos + t1 * sin], axis=-1)
    return jnp.concatenate([rot.astype(t.dtype), t[..., ROT_DIM:]], axis=-1)


def _window_gqa(q, k, v, sink):
    B, L = q.shape[0], q.shape[1]
    nb = L // BLOCK
    qb = q.reshape(B, nb, BLOCK, N_KV_HEADS, GQA_GROUPS, HEAD_DIM)

    def neighbours(t):
        tp = jnp.pad(t, ((0, 0), (BLOCK, BLOCK), (0, 0), (0, 0)))
        return jnp.concatenate(
            [tp[:, i * BLOCK:i * BLOCK + L].reshape(B, nb, BLOCK, N_KV_HEADS, HEAD_DIM) for i in range(3)],
            axis=2)

    kb, vb = neighbours(k), neighbours(v)
    s = jnp.einsum('bnqkgd,bnskd->bnkgqs', qb, kb).astype(jnp.float32) * (HEAD_DIM ** -0.5)
    blk = jnp.arange(nb)[:, None, None]
    qpos = blk * BLOCK + jnp.arange(BLOCK)[None, :, None]
    kpos = (blk - 1) * BLOCK + jnp.arange(3 * BLOCK)[None, None, :]
    valid = (jnp.abs(kpos - qpos) <= WINDOW) & (kpos >= 0) & (kpos < L)
    s = jnp.where(valid[None, :, None, None], s, MASK_VALUE)
    sk = sink.astype(jnp.float32).reshape(N_KV_HEADS, GQA_GROUPS)[:, :, None, None]
    m = jnp.maximum(jnp.max(s, axis=-1, keepdims=True), sk)
    p = jnp.exp(s - m)
    p = p / (jnp.sum(p, axis=-1, keepdims=True) + jnp.exp(sk - m))
    o = jnp.einsum('bnkgqs,bnskd->bnqkgd', p.astype(v.dtype), vb)
    return o.reshape(B, L, N_HEADS * HEAD_DIM)


def _short_conv(u, w, b):
    up = jnp.pad(u, ((0, 0), (1, 1), (0, 0)))
    return up[:, :-2] * w[0] + up[:, 1:-1] * w[1] + up[:, 2:] * w[2] + b


def _hyena_filters(L, w1, b1, f1, w2, b2, f2, w3, decay):
    pos = jnp.arange(L, dtype=jnp.float32)
    t = pos / (L - 1)
    w = 2.0 * math.pi * pos / L
    bands = jnp.linspace(1e-4, FILTER_BANDS - 1, FILTER_BANDS, dtype=jnp.float32)
    ang = w[:, None] * bands[None, :]
    z = jnp.concatenate([t[:, None], jnp.cos(ang), -jnp.sin(ang)], axis=-1)
    h = jnp.sin(f1 * (z @ w1 + b1))
    h = jnp.sin(f2 * (h @ w2 + b2))
    h = (h @ w3).astype(jnp.float32)
    h = h * jnp.exp(-t[:, None] * jnp.abs(decay.astype(jnp.float32))[None, :])
    h = h.reshape(L, HYENA_ORDER, 2, D_HYENA)
    h = h / jnp.sum(jnp.abs(h), axis=(0, 2), keepdims=True)
    fwd, bwd = h[:, :, 0], h[:, :, 1]
    full = jnp.concatenate([fwd, jnp.zeros((1, HYENA_ORDER, D_HYENA), jnp.float32), bwd[:0:-1]], axis=0)
    return jnp.fft.rfft(full, axis=0)


def _fftconv(u, kf, d):
    L = u.shape[1]
    uf = u.astype(jnp.float32)
    y = jnp.fft.irfft(jnp.fft.rfft(uf, n=2 * L, axis=1) * kf[None], n=2 * L, axis=1)[:, :L]
    return (y + uf * d.astype(jnp.float32)).astype(u.dtype)


def _layer(x, w_in, sink, conv_w, conv_b, fw1, fb1, ff1, fw2, fb2, ff2, fw3, fdecay,
           hy_d, attn_g, hy_g, w_out, ln_g, ln_b):
    B, L, _ = x.shape
    proj = jnp.einsum('bld,de->ble', x, w_in)
    q, k, v, g_a, hy, g_h = jnp.split(proj, SPLIT_POINTS, axis=-1)
    q = _partial_rope(q.reshape(B, L, N_HEADS, HEAD_DIM))
    k = _partial_rope(k.reshape(B, L, N_KV_HEADS, HEAD_DIM))
    v = v.reshape(B, L, N_KV_HEADS, HEAD_DIM)
    a = _window_gqa(q, k, v, sink)
    hv, hx1, hx2 = jnp.split(_short_conv(hy, conv_w, conv_b), 3, axis=-1)
    kf = _hyena_filters(L, fw1, fb1, ff1, fw2, fb2, ff2, fw3, fdecay)
    z = hv
    for n, gate in enumerate((hx1, hx2)):
        z = gate * _fftconv(z, kf[:, n], hy_d[n])
    a = _rmsnorm(a, attn_g) * jax.nn.silu(g_a)
    z = _rmsnorm(z, hy_g) * jax.nn.silu(g_h)
    out = jnp.einsum('ble,ed->bld', jnp.concatenate([a, z], axis=-1), w_out)
    return _layernorm(DN_ALPHA * x + out, ln_g, ln_b)


def setup_inputs(seed: int = 0) -> dict:
    key = jax.random.key(seed)
    ks = jax.random.split(key, 24)
    f32 = jnp.float32

    def nrm(k, shape, s):
        return s * jax.random.normal(k, shape, f32)

    col_scale = (jnp.ones((D_IN,), f32)
                 .at[V_START:V_START + KV_COLS].set(DN_BETA)
                 .at[HYV_START:HYV_START + D_HYENA].set(DN_BETA))
    decay_base = jnp.tile(jnp.linspace(DECAY_SLOW, DECAY_FAST, D_HYENA, dtype=f32), HYENA_ORDER * 2)
    return {
        'x_prompt': nrm(ks[0], (BATCH, SEQ, D_MODEL), 1.0),
        'x_sample': nrm(ks[1], (DEC_BATCH, DEC_SEQ, D_MODEL), 1.0),
        'emb_ln_g': 1.0 + nrm(ks[2], (D_MODEL,), 0.02),
        'emb_ln_b': nrm(ks[3], (D_MODEL,), 0.02),
        'w_in': nrm(ks[4], (DEPTH, D_MODEL, D_IN), D_MODEL ** -0.5) * col_scale,
        'attn_sink': nrm(ks[5], (DEPTH, N_HEADS), 0.5),
        'conv_w': nrm(ks[6], (DEPTH, 3, 3 * D_HYENA), 3 ** -0.5),
        'conv_b': nrm(ks[7], (DEPTH, 3 * D_HYENA), 0.02),
        'flt_w1': nrm(ks[8], (DEPTH, FILTER_EMB, FILTER_HIDDEN), FILTER_EMB ** -0.5),
        'flt_b1': nrm(ks[9], (DEPTH, FILTER_HIDDEN), 0.02),
        'flt_freq1': 1.0 + nrm(ks[10], (DEPTH, FILTER_HIDDEN), 0.1),
        'flt_w2': nrm(ks[11], (DEPTH, FILTER_HIDDEN, FILTER_HIDDEN), FILTER_HIDDEN ** -0.5),
        'flt_b2': nrm(ks[12], (DEPTH, FILTER_HIDDEN), 0.02),
        'flt_freq2': 1.0 + nrm(ks[13], (DEPTH, FILTER_HIDDEN), 0.1),
        'flt_w3': nrm(ks[14], (DEPTH, FILTER_HIDDEN, N_FILTERS), FILTER_HIDDEN ** -0.5),
        'flt_decay': decay_base * (1.0 + nrm(ks[15], (DEPTH, N_FILTERS), 0.05)),
        'hyena_d': nrm(ks[16], (DEPTH, HYENA_ORDER, D_HYENA), 0.5),
        'attn_norm_g': 1.0 + nrm(ks[17], (DEPTH, D_ATTN), 0.02),
        'hyena_norm_g': 1.0 + nrm(ks[18], (DEPTH, D_HYENA), 0.02),
        'w_out': nrm(ks[19], (DEPTH, D_MIX, D_MODEL), D_MIX ** -0.5 * DN_BETA),
        'ln_g': 1.0 + nrm(ks[20], (DEPTH, D_MODEL), 0.02),
        'ln_b': nrm(ks[21], (DEPTH, D_MODEL), 0.02),
    }


def reference(x_prompt, x_sample, emb_ln_g, emb_ln_b, w_in, attn_sink, conv_w, conv_b,
              flt_w1, flt_b1, flt_freq1, flt_w2, flt_b2, flt_freq2, flt_w3, flt_decay,
              hyena_d, attn_norm_g, hyena_norm_g, w_out, ln_g, ln_b):
    def trunk(x):
        h = _layernorm(x, emb_ln_g, emb_ln_b)
        for l in range(DEPTH):
            h = _layer(h, w_in[l], attn_sink[l], conv_w[l], conv_b[l],
                       flt_w1[l], flt_b1[l], flt_freq1[l], flt_w2[l], flt_b2[l], flt_freq2[l],
                       flt_w3[l], flt_decay[l], hyena_d[l], attn_norm_g[l], hyena_norm_g[l],
                       w_out[l], ln_g[l], ln_b[l])
        return h

    y_prompt = trunk(x_prompt)
    y_sample = trunk(x_sample)
    return (y_prompt, y_sample)
```

```python
import functools
import math

import jax
import jax.numpy as jnp
from jax.experimental import pallas as pl
from jax.experimental.pallas import tpu as pltpu

F32 = jnp.float32
BF16 = jnp.bfloat16

D_MODEL = 2048
DEPTH = 2
D_ATTN = 1024
D_HYENA = 1024
HEAD_DIM = 64
N_HEADS = 16
N_KV_HEADS = 4
GQA_GROUPS = 4
ROT_DIM = 16
ROPE_THETA = 500000.0
WINDOW = 128
BLOCK = 128
FILTER_BANDS = 16
FILTER_HIDDEN = 64
DN_ALPHA = (2.0 * DEPTH) ** 0.25
NORM_EPS = 1e-5
MASK_VALUE = -1e30
Q_COLS = N_HEADS * HEAD_DIM
KV_COLS = N_KV_HEADS * HEAD_DIM
D_IN = 2 * Q_COLS + 2 * KV_COLS + 4 * D_HYENA

COL_Q, COL_GA, COL_GH, COL_HV, COL_HX1, COL_HX2 = 0, 1, 2, 3, 4, 5
COL_K, COL_V = 24, 25

ROWS_PER_GROUP = 8
VMEM_LIMIT = 48 * 1024 * 1024


def _params(*sem):
    return pltpu.CompilerParams(dimension_semantics=sem, vmem_limit_bytes=VMEM_LIMIT)


def _ln_kernel(x_ref, g_ref, b_ref, o_ref, ob_ref):
    x = x_ref[...]
    mu = jnp.mean(x, axis=-1, keepdims=True)
    xc = x - mu
    var = jnp.mean(xc * xc, axis=-1, keepdims=True)
    y = xc * jax.lax.rsqrt(var + NORM_EPS) * g_ref[...] + b_ref[...]
    o_ref[...] = y
    ob_ref[...] = y.astype(BF16)


def _layernorm(x, g, b, tm=512):
    T, D = x.shape
    return pl.pallas_call(
        _ln_kernel,
        grid=(T // tm,),
        in_specs=[pl.BlockSpec((tm, D), lambda i: (i, 0)),
                  pl.BlockSpec((1, D), lambda i: (0, 0)),
                  pl.BlockSpec((1, D), lambda i: (0, 0))],
        out_specs=[pl.BlockSpec((tm, D), lambda i: (i, 0)),
                   pl.BlockSpec((tm, D), lambda i: (i, 0))],
        out_shape=[jax.ShapeDtypeStruct((T, D), F32), jax.ShapeDtypeStruct((T, D), BF16)],
        compiler_params=_params("parallel"),
        name="layernorm",
    )(x, g.reshape(1, D), b.reshape(1, D))


def _mm_kernel(x_ref, w_ref, o_ref):
    o_ref[...] = jnp.dot(x_ref[...], w_ref[...], preferred_element_type=F32).astype(o_ref.dtype)


def _matmul(x, w, tm, tn, out_dtype):
    M, K = x.shape
    N = w.shape[1]
    return pl.pallas_call(
        _mm_kernel,
        grid=(M // tm, N // tn),
        in_specs=[pl.BlockSpec((tm, K), lambda i, j: (i, 0)),
                  pl.BlockSpec((K, tn), lambda i, j: (0, j))],
        out_specs=pl.BlockSpec((tm, tn), lambda i, j: (i, j)),
        out_shape=jax.ShapeDtypeStruct((M, N), out_dtype),
        compiler_params=_params("parallel", "parallel"),
        name="in_proj",
    )(x, w)


def _rope(x, tab):
    c, s_lo, s_hi = tab[0], tab[1], tab[2]
    pieces = []
    for cg in range(x.shape[1] // 128):
        t = x[:, cg * 128:(cg + 1) * 128]
        pieces.append(t * c + pltpu.roll(t, 128 - ROT_DIM // 2, 1) * s_lo
                      + pltpu.roll(t, ROT_DIM // 2, 1) * s_hi)
    return jnp.concatenate(pieces, axis=1) if len(pieces) > 1 else pieces[0]


def _attn_kernel(sink_ref, q_ref, ga_ref, kp_ref, kc_ref, kn_ref, vp_ref, vc_ref, vn_ref,
                 tp_ref, tc_ref, tn_ref, g_ref, o_ref, acc_ref):
    n = pl.program_id(1)
    nb = pl.num_programs(1)
    qs = (_rope(q_ref[...].astype(F32), tc_ref[...]) * (HEAD_DIM ** -0.5)).astype(BF16)
    k3 = jnp.concatenate([
        _rope(kp_ref[...].astype(F32), tp_ref[...]),
        _rope(kc_ref[...].astype(F32), tc_ref[...]),
        _rope(kn_ref[...].astype(F32), tn_ref[...])], axis=0).astype(BF16)
    v3 = jnp.concatenate([vp_ref[...], vc_ref[...], vn_ref[...]], axis=0)

    r = jax.lax.broadcasted_iota(jnp.int32, (BLOCK, 3 * BLOCK), 0)
    c = jax.lax.broadcasted_iota(jnp.int32, (BLOCK, 3 * BLOCK), 1)
    valid = (c >= r) & (c <= r + 2 * WINDOW)
    valid = valid & ((c >= BLOCK) | (n > 0)) & ((c < 2 * BLOCK) | (n < nb - 1))

    for g in range(N_KV_HEADS):
        kg = k3[:, g * HEAD_DIM:(g + 1) * HEAD_DIM]
        vg = v3[:, g * HEAD_DIM:(g + 1) * HEAD_DIM]
        heads = [g * GQA_GROUPS + hh for hh in range(GQA_GROUPS)]
        qg = jnp.concatenate([qs[:, h * HEAD_DIM:(h + 1) * HEAD_DIM] for h in heads], axis=0)
        s = jax.lax.dot_general(qg, kg, (((1,), (1,)), ((), ())), preferred_element_type=F32)
        s = jnp.where(valid[None], s.reshape(GQA_GROUPS, BLOCK, 3 * BLOCK), MASK_VALUE)
        sk = jnp.concatenate([jnp.full((1, BLOCK, 1), sink_ref[h], F32) for h in heads], axis=0)
        m = jnp.maximum(jnp.max(s, axis=-1, keepdims=True), sk)
        p = jnp.exp(s - m)
        denom = jnp.sum(p, axis=-1, keepdims=True) + jnp.exp(sk - m)
        o = jnp.dot(p.reshape(GQA_GROUPS * BLOCK, 3 * BLOCK).astype(BF16), vg,
                    preferred_element_type=F32)
        o = o / denom.reshape(GQA_GROUPS * BLOCK, 1)
        for hh, h in enumerate(heads):
            acc_ref[:, h * HEAD_DIM:(h + 1) * HEAD_DIM] = o[hh * BLOCK:(hh + 1) * BLOCK]

    a = acc_ref[...]
    an = a * jax.lax.rsqrt(jnp.mean(a * a, axis=-1, keepdims=True) + NORM_EPS) * g_ref[...]
    ga = ga_ref[...].astype(F32)
    o_ref[...] = (an * (ga * jax.nn.sigmoid(ga))).astype(o_ref.dtype)


def _attention(proj3, rope_tab, sink, attn_g):
    B, L, _ = proj3.shape
    nb = L // BLOCK
    prev = lambda b, n: (b, jnp.maximum(n - 1, 0))
    nxt = lambda b, n: (b, jnp.minimum(n + 1, nb - 1))

    def kv_spec(col, where):
        if where == 0:
            return pl.BlockSpec((None, BLOCK, KV_COLS), lambda b, n: (*prev(b, n), col))
        if where == 1:
            return pl.BlockSpec((None, BLOCK, KV_COLS), lambda b, n: (b, n, col))
        return pl.BlockSpec((None, BLOCK, KV_COLS), lambda b, n: (*nxt(b, n), col))

    def tab_spec(where):
        if where == 0:
            return pl.BlockSpec((3, BLOCK, 128), lambda b, n: (0, jnp.maximum(n - 1, 0), 0))
        if where == 1:
            return pl.BlockSpec((3, BLOCK, 128), lambda b, n: (0, n, 0))
        return pl.BlockSpec((3, BLOCK, 128), lambda b, n: (0, jnp.minimum(n + 1, nb - 1), 0))

    return pl.pallas_call(
        _attn_kernel,
        grid=(B, nb),
        in_specs=[pl.BlockSpec(memory_space=pltpu.SMEM),
                  pl.BlockSpec((None, BLOCK, Q_COLS), lambda b, n: (b, n, COL_Q)),
                  pl.BlockSpec((None, BLOCK, D_ATTN), lambda b, n: (b, n, COL_GA)),
                  kv_spec(COL_K, 0), kv_spec(COL_K, 1), kv_spec(COL_K, 2),
                  kv_spec(COL_V, 0), kv_spec(COL_V, 1), kv_spec(COL_V, 2),
                  tab_spec(0), tab_spec(1), tab_spec(2),
                  pl.BlockSpec((1, D_ATTN), lambda b, n: (0, 0))],
        out_specs=pl.BlockSpec((None, BLOCK, D_ATTN), lambda b, n: (b, n, 0)),
        out_shape=jax.ShapeDtypeStruct((B, L, D_ATTN), BF16),
        scratch_shapes=[pltpu.VMEM((BLOCK, D_ATTN), F32)],
        compiler_params=_params("parallel", "parallel"),
        name="window_attention",
    )(sink, proj3, proj3, proj3, proj3, proj3, proj3, proj3, proj3,
      rope_tab, rope_tab, rope_tab, attn_g.reshape(1, D_ATTN))


def _rope_table(L):
    inv = ROPE_THETA ** (-jnp.arange(0, ROT_DIM, 2, dtype=F32) / ROT_DIM)
    ang = jnp.arange(L, dtype=F32)[:, None] * inv[None, :]
    cos, sin = jnp.cos(ang), jnp.sin(ang)
    half = ROT_DIM // 2
    rest = HEAD_DIM - ROT_DIM
    c = jnp.concatenate([cos, cos, jnp.ones((L, rest), F32)], axis=1)
    s_lo = jnp.concatenate([-sin, jnp.zeros((L, HEAD_DIM - half), F32)], axis=1)
    s_hi = jnp.concatenate([jnp.zeros((L, half), F32), sin, jnp.zeros((L, rest), F32)], axis=1)
    tab = jnp.stack([c, s_lo, s_hi])
    return jnp.concatenate([tab, tab], axis=2)


def _sconv_kernel(x_ref, w_ref, b_ref, o_ref, pad_ref, *, chunk):
    L, cb = x_ref.shape
    pad_ref[0:8, :] = jnp.zeros((8, cb), F32)
    pad_ref[L + 8:L + 16, :] = jnp.zeros((8, cb), F32)
    for r0 in range(0, L, chunk):
        pad_ref[8 + r0:8 + r0 + chunk, :] = x_ref[r0:r0 + chunk, :].astype(F32)
    w0, w1, w2, b = w_ref[0:1, :], w_ref[1:2, :], w_ref[2:3, :], b_ref[...]
    for r0 in range(0, L, chunk):
        xm = pad_ref[7 + r0:7 + r0 + chunk, :]
        x0 = pad_ref[8 + r0:8 + r0 + chunk, :]
        xp = pad_ref[9 + r0:9 + r0 + chunk, :]
        o_ref[r0:r0 + chunk, :] = xm * w0 + x0 * w1 + xp * w2 + b


def _short_conv(proj3, conv_w, conv_b, cb=256):
    B, L, _ = proj3.shape
    ncb = 3 * D_HYENA // cb
    off = COL_HV * 1024 // cb
    return pl.pallas_call(
        functools.partial(_sconv_kernel, chunk=min(L, 512)),
        grid=(B, ncb),
        in_specs=[pl.BlockSpec((None, L, cb), lambda b, c: (b, 0, off + c)),
                  pl.BlockSpec((3, cb), lambda b, c: (0, c)),
                  pl.BlockSpec((1, cb), lambda b, c: (0, c))],
        out_specs=pl.BlockSpec((None, L, cb), lambda b, c: (b, 0, c)),
        out_shape=jax.ShapeDtypeStruct((B, L, 3 * D_HYENA), F32),
        scratch_shapes=[pltpu.VMEM((L + 16, cb), F32)],
        compiler_params=_params("parallel", "parallel"),
        name="short_conv",
    )(proj3, conv_w, conv_b.reshape(1, -1))


def _split(a):
    hi = a.astype(BF16)
    return hi, (a - hi.astype(F32)).astype(BF16)


def _dot3(a, b):
    ah, al = _split(a)
    bh, bl = _split(b)
    d = lambda x, y: jnp.dot(x, y, preferred_element_type=F32)
    return d(ah, bh) + (d(ah, bl) + d(al, bh))


def _fmlp_kernel(bands_ref, w1t_ref, w1c_ref, w1s_ref, b1_ref, f1_ref, w2_ref, b2_ref, f2_ref,
                 w3f_ref, w3b_ref, df_ref, db_ref, of_ref, ob_ref, nrm_ref, *, L, tr):
    i = pl.program_id(0)
    pos = (i * tr + jax.lax.broadcasted_iota(jnp.int32, (tr, 1), 0)).astype(F32)

    def hidden(p):
        t = p / (L - 1)
        w = 2.0 * math.pi * p / L
        ang = w * bands_ref[...]
        pre = (t * w1t_ref[...] + _dot3(jnp.cos(ang), w1c_ref[...])
               + _dot3(-jnp.sin(ang), w1s_ref[...]) + b1_ref[...])
        h = jnp.sin(f1_ref[...] * pre)
        h = jnp.sin(f2_ref[...] * (_dot3(h, w2_ref[...]) + b2_ref[...]))
        return h, t

    h_a, t_a = hidden(pos)
    h_b, t_b = hidden(pos + 1.0)
    dec_b = jnp.abs(db_ref[...])
    out_f = _dot3(h_a, w3f_ref[...]) * jnp.exp(-t_a * jnp.abs(df_ref[...]))
    out_b = _dot3(h_b, w3b_ref[...]) * jnp.exp(-t_b * dec_b)
    out_b = jnp.where(pos + 1.0 <= L - 1, out_b, 0.0)
    of_ref[...] = out_f
    ob_ref[...] = out_b
    part = (jnp.sum(jnp.abs(out_f), axis=0, keepdims=True)
            + jnp.sum(jnp.abs(out_b), axis=0, keepdims=True))

    @pl.when(i == 0)
    def _():
        b0 = _dot3(h_a[0:8], w3b_ref[...]) * jnp.exp(-t_a[0:8] * dec_b)
        nrm_ref[...] = part + jnp.abs(b0[0:1])

    @pl.when(i > 0)
    def _():
        nrm_ref[...] += part


def _filter_mlp(L, w1, b1, f1, w2, b2, f2, w3, decay, tr=256):
    nf = 2 * D_HYENA
    w3r = w3.reshape(FILTER_HIDDEN, 2, 2, D_HYENA)
    dr = decay.reshape(2, 2, D_HYENA)
    w3f, w3b = w3r[:, :, 0].reshape(FILTER_HIDDEN, nf), w3r[:, :, 1].reshape(FILTER_HIDDEN, nf)
    df, db = dr[:, 0].reshape(1, nf), dr[:, 1].reshape(1, nf)
    bands = jnp.linspace(1e-4, FILTER_BANDS - 1, FILTER_BANDS, dtype=F32).reshape(1, FILTER_BANDS)
    H = FILTER_HIDDEN
    full = lambda shape: pl.BlockSpec(shape, lambda i: (0,) * len(shape))
    return pl.pallas_call(
        functools.partial(_fmlp_kernel, L=L, tr=tr),
        grid=(L // tr,),
        in_specs=[full((1, FILTER_BANDS)), full((1, H)), full((FILTER_BANDS, H)),
                  full((FILTER_BANDS, H)), full((1, H)), full((1, H)), full((H, H)),
                  full((1, H)), full((1, H)), full((H, nf)), full((H, nf)),
                  full((1, nf)), full((1, nf))],
        out_specs=[pl.BlockSpec((tr, nf), lambda i: (i, 0)),
                   pl.BlockSpec((tr, nf), lambda i: (i, 0)),
                   pl.BlockSpec((1, nf), lambda i: (0, 0))],
        out_shape=[jax.ShapeDtypeStruct((L, nf), F32), jax.ShapeDtypeStruct((L, nf), F32),
                   jax.ShapeDtypeStruct((1, nf), F32)],
        compiler_params=_params("arbitrary"),
        name="filter_mlp",
    )(bands, w1[0:1], w1[1:1 + FILTER_BANDS], w1[1 + FILTER_BANDS:], b1.reshape(1, H),
      f1.reshape(1, H), w2, b2.reshape(1, H), f2.reshape(1, H), w3f, w3b, df, db)


def _trig(num, den):
    ang = (2.0 * math.pi / den) * (num % den).astype(F32)
    return jnp.cos(ang), jnp.sin(ang)


def _kron_rows(f, perm=None):
    eye = jnp.eye(ROWS_PER_GROUP, dtype=F32) if perm is None else perm
    return jnp.kron(f, eye).astype(BF16)


def _dft_constants(L, pairs):
    n = 2 * L
    n1 = 128 if L >= 8192 else 64
    n2 = n // n1
    n1h = n1 // 2
    q = n1h if pairs else n1 // 4
    ar = lambda m: jnp.arange(m, dtype=jnp.int32)

    c, s = _trig(ar(n1)[:, None] * ar(q)[None, :], n1)
    f1 = jnp.stack([jnp.concatenate([c, s], axis=1), jnp.concatenate([-s, c], axis=1)], axis=1)
    k1 = _kron_rows(f1.reshape(2 * n1, 2 * q))

    c, s = _trig(ar(n1)[:, None] * ar(n1h)[None, :], n1)
    ff = jnp.stack([c, -s], axis=1).reshape(2 * n1, n1h)
    c, s = _trig(ar(n1)[:, None] * (n1 - 1 - ar(n1h))[None, :], n1)
    fb = jnp.stack([c, -s], axis=1).reshape(2 * n1, n1h)
    k1f = _kron_rows(ff)
    k1b = _kron_rows(fb, jnp.eye(ROWS_PER_GROUP, dtype=F32)[::-1])

    kk = ar(n1)[:, None, None] + n1 * ar(n2)[None, :, None]
    c, s = _trig(ar(n2)[None, None, :] * kk, n)
    gr, gi = c, -s
    gs = jnp.concatenate([jnp.concatenate([gr, -gi], axis=2),
                          jnp.concatenate([gi, gr], axis=2)], axis=1)
    gst = jnp.swapaxes(gs, 1, 2)

    c, s = _trig(ar(n1h)[:, None] * ar(n1)[None, :], n1)
    if pairs:
        top = jnp.stack([c, -s], axis=2).reshape(n1h, 2 * n1)
        bot = jnp.stack([s, c], axis=2).reshape(n1h, 2 * n1)
        f3 = jnp.concatenate([top, bot], axis=0)
    else:
        c2, s2 = _trig((ar(n1h)[:, None] - q) * ar(n1)[None, :], n1)
        f3 = jnp.stack([c + s2, -s + c2], axis=2).reshape(n1h, 2 * n1)
    k3 = _kron_rows(f3 / n)
    return dict(n1=n1, n2=n2, q=q, k1=k1, k1f=k1f, k1b=k1b, gs=gs.astype(BF16),
                gst=gst.astype(BF16), k3=k3)


R = ROWS_PER_GROUP


def _p1_kernel(x_ref, k_ref, o_ref):
    n1, _, _, cb = o_ref.shape
    x = x_ref[...].reshape(-1, cb).astype(BF16)
    res = jnp.dot(k_ref[...], x, preferred_element_type=F32)
    o_ref[...] = res.reshape(n1, 2, R, cb)


def _dft_outer(u6, col_off, k1, n1, cb):
    bp, _, q, n2g, _, _ = u6.shape
    C = D_HYENA
    return pl.pallas_call(
        _p1_kernel,
        grid=(bp, C // cb, n2g),
        in_specs=[pl.BlockSpec((None, 2, q, None, R, cb), lambda b, c, a: (b, 0, 0, a, 0, col_off + c)),
                  pl.BlockSpec(k1.shape, lambda b, c, a: (0, 0))],
        out_specs=pl.BlockSpec((None, n1, 2, None, R, cb), lambda b, c, a: (b, 0, 0, a, 0, c)),
        out_shape=jax.ShapeDtypeStruct((bp, n1, 2, n2g, R, C), F32),
        compiler_params=_params("parallel", "parallel", "parallel"),
        name="dft_outer",
    )(u6, k1)


def _p1f_kernel(xf_ref, xb_ref, kf_ref, kb_ref, o_ref):
    n1, _, _, cb = o_ref.shape
    xf = xf_ref[...].reshape(-1, cb).astype(BF16)
    xb = xb_ref[...].reshape(-1, cb).astype(BF16)
    res = (jnp.dot(kf_ref[...], xf, preferred_element_type=F32)
           + jnp.dot(kb_ref[...], xb, preferred_element_type=F32))
    o_ref[...] = res.reshape(n1, 2, R, cb)


def _filter_outer(hf, hb, k1f, k1b, n1, n2, cb):
    L, nf = hf.shape
    n1h, n2g = n1 // 2, n2 // R
    hf4 = hf.reshape(n1h, n2g, R, nf)
    hb4 = hb.reshape(n1h, n2g, R, nf)
    return pl.pallas_call(
        _p1f_kernel,
        grid=(nf // cb, n2g),
        in_specs=[pl.BlockSpec((n1h, None, R, cb), lambda c, a: (0, a, 0, c)),
                  pl.BlockSpec((n1h, None, R, cb), lambda c, a: (0, n2g - 1 - a, 0, c)),
                  pl.BlockSpec(k1f.shape, lambda c, a: (0, 0)),
                  pl.BlockSpec(k1b.shape, lambda c, a: (0, 0))],
        out_specs=pl.BlockSpec((n1, 2, None, R, cb), lambda c, a: (0, 0, a, 0, c)),
        out_shape=jax.ShapeDtypeStruct((n1, 2, n2g, R, nf), F32),
        compiler_params=_params("parallel", "parallel"),
        name="filter_outer",
    )(hf4, hb4, k1f, k1b)


def _p2f_kernel(a_ref, g_ref, nrm_ref, o_ref):
    kb, _, n2, cb = a_ref.shape
    inv = 1.0 / nrm_ref[...]
    for kk in range(kb):
        a = a_ref[kk].reshape(2 * n2, cb).astype(BF16)
        u = jnp.dot(g_ref[kk], a, preferred_element_type=F32) * inv
        o_ref[kk] = u.reshape(2, n2, cb)


def _filter_inner(af, gs, nrm, kb, cb):
    n1, _, n2, nf = af.shape
    return pl.pallas_call(
        _p2f_kernel,
        grid=(nf // cb, n1 // kb),
        in_specs=[pl.BlockSpec((kb, 2, n2, cb), lambda c, k: (k, 0, 0, c)),
                  pl.BlockSpec((kb, 2 * n2, 2 * n2), lambda c, k: (k, 0, 0)),
                  pl.BlockSpec((1, cb), lambda c, k: (0, c))],
        out_specs=pl.BlockSpec((kb, 2, n2, cb), lambda c, k: (k, 0, 0, c)),
        out_shape=jax.ShapeDtypeStruct((n1, 2, n2, nf), F32),
        compiler_params=_params("parallel", "parallel"),
        name="filter_inner",
    )(af, gs, nrm)


def _p2_kernel(a_ref, kf_ref, g_ref, gt_ref, o_ref):
    kb, _, n2, cb = a_ref.shape
    for kk in range(kb):
        a = a_ref[kk].reshape(2 * n2, cb).astype(BF16)
        u = jnp.dot(g_ref[kk], a, preferred_element_type=F32)
        ur, ui = u[:n2], u[n2:]
        kr, ki = kf_ref[kk, 0], kf_ref[kk, 1]
        vr = (ur * kr - ui * ki).astype(BF16)
        vi = (ur * ki + ui * kr).astype(BF16)
        gt = gt_ref[kk]
        b = (jnp.dot(gt[:, :n2], vr, preferred_element_type=F32)
             + jnp.dot(gt[:, n2:], vi, preferred_element_type=F32))
        o_ref[kk] = b.reshape(2, n2, cb)


def _dft_inner(a5, kf, order, gs, gst, kb, cb):
    bp, n1, _, n2, C = a5.shape
    koff = order * (C // cb)
    return pl.pallas_call(
        _p2_kernel,
        grid=(bp, C // cb, n1 // kb),
        in_specs=[pl.BlockSpec((None, kb, 2, n2, cb), lambda b, c, k: (b, k, 0, 0, c)),
                  pl.BlockSpec((kb, 2, n2, cb), lambda b, c, k: (k, 0, 0, koff + c)),
                  pl.BlockSpec((kb, 2 * n2, 2 * n2), lambda b, c, k: (k, 0, 0)),
                  pl.BlockSpec((kb, 2 * n2, 2 * n2), lambda b, c, k: (k, 0, 0))],
        out_specs=pl.BlockSpec((None, kb, 2, n2, cb), lambda b, c, k: (b, k, 0, 0, c)),
        out_shape=jax.ShapeDtypeStruct(a5.shape, F32),
        compiler_params=_params("parallel", "parallel", "parallel"),
        name="dft_inner",
    )(a5, kf, gs, gst)


def _p3_kernel(b_ref, k_ref, u_ref, g_ref, d_ref, o_ref):
    _, q, _, cb = o_ref.shape
    b = b_ref[...].reshape(-1, cb).astype(BF16)
    y = jnp.dot(k_ref[...], b, preferred_element_type=F32)
    u = u_ref[...].reshape(-1, cb)
    g = g_ref[...].reshape(-1, cb)
    o_ref[...] = (g * (y + u * d_ref[...])).reshape(2, q, R, cb)


def _dft_outer_inv(b6, k3, u6, u_off, g6, g_off, d, cb):
    bp, n1, _, n2g, _, C = b6.shape
    q = u6.shape[2]
    tspec = lambda off: pl.BlockSpec((None, 2, q, None, R, cb),
                                     lambda b, c, a: (b, 0, 0, a, 0, off + c))
    return pl.pallas_call(
        _p3_kernel,
        grid=(bp, C // cb, n2g),
        in_specs=[pl.BlockSpec((None, n1, 2, None, R, cb), lambda b, c, a: (b, 0, 0, a, 0, c)),
                  pl.BlockSpec(k3.shape, lambda b, c, a: (0, 0)),
                  tspec(u_off), tspec(g_off),
                  pl.BlockSpec((1, cb), lambda b, c, a: (0, c))],
        out_specs=tspec(0),
        out_shape=jax.ShapeDtypeStruct((bp, 2, q, n2g, R, C), F32),
        compiler_params=_params("parallel", "parallel", "parallel"),
        name="dft_outer_inv",
    )(b6, k3, u6, g6, d.reshape(1, C))


def _hyena(hyc, kf, dc, hy_d, cb=512):
    B, L, _ = hyc.shape
    C = D_HYENA
    n1, n2, q = dc["n1"], dc["n2"], dc["q"]
    bp = B * L // (2 * q * n2)
    n2g = n2 // R
    kb = 4 if n2 <= 64 else 2
    hy6 = hyc.reshape(bp, 2, q, n2g, R, 3 * C)
    z6, z_off = hy6, 0
    for order in range(2):
        a6 = _dft_outer(z6, z_off, dc["k1"], n1, cb)
        b5 = _dft_inner(a6.reshape(bp, n1, 2, n2, C), kf, order, dc["gs"], dc["gst"], kb, cb)
        z6 = _dft_outer_inv(b5.reshape(bp, n1, 2, n2g, R, C), dc["k3"], z6, z_off,
                            hy6, (1 + order) * (C // cb), hy_d[order], cb)
        z_off = 0
    return z6.reshape(B, L, C)


def _filter_spectrum(L, dc, fw1, fb1, ff1, fw2, fb2, ff2, fw3, fdecay, cb=512):
    hf, hb, nrm = _filter_mlp(L, fw1, fb1, ff1, fw2, fb2, ff2, fw3, fdecay)
    n1, n2 = dc["n1"], dc["n2"]
    af = _filter_outer(hf, hb, dc["k1f"], dc["k1b"], n1, n2, cb)
    kb = 4 if n2 <= 64 else 2
    return _filter_inner(af.reshape(n1, 2, n2, 2 * D_HYENA), dc["gs"], nrm, kb, cb)


def _out_kernel(a_ref, z_ref, gh_ref, hg_ref, wa_ref, wz_ref, h_ref, g_ref, b_ref, o_ref, ob_ref):
    z = z_ref[...]
    gh = gh_ref[...].astype(F32)
    zn = z * jax.lax.rsqrt(jnp.mean(z * z, axis=-1, keepdims=True) + NORM_EPS) * hg_ref[...]
    zn = (zn * (gh * jax.nn.sigmoid(gh))).astype(BF16)
    acc = (jnp.dot(a_ref[...], wa_ref[...], preferred_element_type=F32)
           + jnp.dot(zn, wz_ref[...], preferred_element_type=F32))
    y = DN_ALPHA * h_ref[...] + acc
    mu = jnp.mean(y, axis=-1, keepdims=True)
    yc = y - mu
    var = jnp.mean(yc * yc, axis=-1, keepdims=True)
    out = yc * jax.lax.rsqrt(var + NORM_EPS) * g_ref[...] + b_ref[...]
    o_ref[...] = out
    ob_ref[...] = out.astype(BF16)


def _out_proj(a, z, proj, hy_g, w_out, h, ln_g, ln_b, tm=256):
    T, D = h.shape
    wa = w_out[:D_ATTN].astype(BF16)
    wz = w_out[D_ATTN:].astype(BF16)
    row = lambda w: pl.BlockSpec((tm, w), lambda i: (i, 0))
    const = lambda r, w: pl.BlockSpec((r, w), lambda i: (0, 0))
    return pl.pallas_call(
        _out_kernel,
        grid=(T // tm,),
        in_specs=[row(D_ATTN), row(D_HYENA),
                  pl.BlockSpec((tm, D_HYENA), lambda i: (i, COL_GH)),
                  const(1, D_HYENA), const(D_ATTN, D), const(D_HYENA, D), row(D),
                  const(1, D), const(1, D)],
        out_specs=[row(D), row(D)],
        out_shape=[jax.ShapeDtypeStruct((T, D), F32), jax.ShapeDtypeStruct((T, D), BF16)],
        compiler_params=_params("parallel"),
        name="out_proj",
    )(a, z, proj, hy_g.reshape(1, -1), wa, wz, h, ln_g.reshape(1, D), ln_b.reshape(1, D))


def _arrange_w_in(w):
    q, k, v, ga, hy, gh = jnp.split(w, [1024, 1280, 1536, 2560, 5632], axis=1)
    return jnp.concatenate([q, ga, gh, hy, k, v], axis=1).astype(BF16)


def _trunk(x, p):
    B, L, D = x.shape
    T = B * L
    pairs = B > 1
    dc = _dft_constants(L, pairs)
    rope_tab = _rope_table(L)
    h, hb = _layernorm(x.reshape(T, D), p["emb_ln_g"], p["emb_ln_b"])
    for l in range(DEPTH):
        proj = _matmul(hb, _arrange_w_in(p["w_in"][l]), 1024, 512, BF16)
        proj3 = proj.reshape(B, L, D_IN)
        a = _attention(proj3, rope_tab, p["attn_sink"][l], p["attn_norm_g"][l])
        hyc = _short_conv(proj3, p["conv_w"][l], p["conv_b"][l])
        kf = _filter_spectrum(L, dc, p["flt_w1"][l], p["flt_b1"][l], p["flt_freq1"][l],
                              p["flt_w2"][l], p["flt_b2"][l], p["flt_freq2"][l],
                              p["flt_w3"][l], p["flt_decay"][l])
        z = _hyena(hyc, kf, dc, p["hyena_d"][l])
        h, hb = _out_proj(a.reshape(T, D_ATTN), z.reshape(T, D_HYENA), proj,
                          p["hyena_norm_g"][l], p["w_out"][l], h, p["ln_g"][l], p["ln_b"][l])
    return h.reshape(B, L, D)


def kernel(x_prompt, x_sample, emb_ln_g, emb_ln_b, w_in, attn_sink, conv_w, conv_b, flt_w1,
           flt_b1, flt_freq1, flt_w2, flt_b2, flt_freq2, flt_w3, flt_decay, hyena_d,
           attn_norm_g, hyena_norm_g, w_out, ln_g, ln_b):
    p = dict(emb_ln_g=emb_ln_g, emb_ln_b=emb_ln_b, w_in=w_in, attn_sink=attn_sink,
             conv_w=conv_w, conv_b=conv_b, flt_w1=flt_w1, flt_b1=flt_b1, flt_freq1=flt_freq1,
             flt_w2=flt_w2, flt_b2=flt_b2, flt_freq2=flt_freq2, flt_w3=flt_w3,
             flt_decay=flt_decay, hyena_d=hyena_d, attn_norm_g=attn_norm_g,
             hyena_norm_g=hyena_norm_g, w_out=w_out, ln_g=ln_g, ln_b=ln_b)
    return (_trunk(x_prompt, p), _trunk(x_sample, p))
```

```python
import functools
import math

import numpy as np
import jax
import jax.numpy as jnp
from jax.experimental import pallas as pl
from jax.experimental.pallas import tpu as pltpu

F32 = jnp.float32
BF16 = jnp.bfloat16

D_MODEL = 2048
DEPTH = 2
D_ATTN = 1024
D_HYENA = 1024
HEAD_DIM = 64
N_HEADS = 16
N_KV_HEADS = 4
GQA_GROUPS = 4
ROT_DIM = 16
ROPE_THETA = 500000.0
WINDOW = 128
BLOCK = 128
FILTER_BANDS = 16
FILTER_HIDDEN = 64
DN_ALPHA = (2.0 * DEPTH) ** 0.25
NORM_EPS = 1e-5
MASK_VALUE = -1e30
Q_COLS = N_HEADS * HEAD_DIM
KV_COLS = N_KV_HEADS * HEAD_DIM
D_IN = 2 * Q_COLS + 2 * KV_COLS + 4 * D_HYENA

COL_Q, COL_GA, COL_GH, COL_HV, COL_HX1, COL_HX2 = 0, 1, 2, 3, 4, 5
COL_K, COL_V = 24, 25

LANES = 128
SUB = 8
PACK = 16
VMEM_LIMIT = 48 * 1024 * 1024


def _params(*sem):
    return pltpu.CompilerParams(dimension_semantics=sem, vmem_limit_bytes=VMEM_LIMIT)


def _ln_kernel(x_ref, g_ref, b_ref, o_ref, ob_ref):
    x = x_ref[...]
    mu = jnp.mean(x, axis=-1, keepdims=True)
    xc = x - mu
    var = jnp.mean(xc * xc, axis=-1, keepdims=True)
    y = xc * jax.lax.rsqrt(var + NORM_EPS) * g_ref[...] + b_ref[...]
    o_ref[...] = y
    ob_ref[...] = y.astype(BF16)


def _layernorm(x, g, b, tm=512):
    T, D = x.shape
    return pl.pallas_call(
        _ln_kernel,
        grid=(T // tm,),
        in_specs=[pl.BlockSpec((tm, D), lambda i: (i, 0)),
                  pl.BlockSpec((1, D), lambda i: (0, 0)),
                  pl.BlockSpec((1, D), lambda i: (0, 0))],
        out_specs=[pl.BlockSpec((tm, D), lambda i: (i, 0)),
                   pl.BlockSpec((tm, D), lambda i: (i, 0))],
        out_shape=[jax.ShapeDtypeStruct((T, D), F32), jax.ShapeDtypeStruct((T, D), BF16)],
        compiler_params=_params("parallel"),
        name="layernorm",
    )(x, g.reshape(1, D), b.reshape(1, D))


def _mm_kernel(x_ref, w_ref, o_ref):
    o_ref[...] = jnp.dot(x_ref[...], w_ref[...], preferred_element_type=F32).astype(o_ref.dtype)


def _matmul(x, w, tm, tn, out_dtype):
    M, K = x.shape
    N = w.shape[1]
    return pl.pallas_call(
        _mm_kernel,
        grid=(M // tm, N // tn),
        in_specs=[pl.BlockSpec((tm, K), lambda i, j: (i, 0)),
                  pl.BlockSpec((K, tn), lambda i, j: (0, j))],
        out_specs=pl.BlockSpec((tm, tn), lambda i, j: (i, j)),
        out_shape=jax.ShapeDtypeStruct((M, N), out_dtype),
        compiler_params=_params("parallel", "parallel"),
        name="in_proj",
    )(x, w)


_SLOT_OF_OCTET = (0, 4, 1, 2, 3, 5, 6, 7)


def _head_pair_lane_perm():
    perm = np.zeros(LANES, np.int32)
    for octet, slot in enumerate(_SLOT_OF_OCTET):
        for hd in range(2):
            for i in range(8):
                perm[16 * slot + 8 * hd + i] = hd * HEAD_DIM + 8 * octet + i
    return perm


def _rope(t, tab):
    return t * tab[0] + pltpu.roll(t, LANES // 2, 1) * tab[1]


def _attn_kernel(sink_ref, q_ref, ga_ref, kp_ref, kc_ref, kn_ref, vp_ref, vc_ref, vn_ref,
                 tp_ref, tc_ref, tn_ref, bias_ref, g_ref, o_ref, acc_ref):
    nkeys = 3 * BLOCK
    tabs = (tp_ref[...], tc_ref[...], tn_ref[...])
    krefs = (kp_ref, kc_ref, kn_ref)
    v3 = jnp.concatenate([vp_ref[...], vc_ref[...], vn_ref[...]], axis=0)
    lane = jax.lax.broadcasted_iota(jnp.int32, (1, LANES), 1)
    head_a = (lane % 16) < 8
    first_pair = jax.lax.broadcasted_iota(jnp.int32, (1, 2 * BLOCK), 1) < BLOCK
    scale = HEAD_DIM ** -0.5
    kcols = [jnp.concatenate(
        [_rope(krefs[w][:, c * LANES:(c + 1) * LANES].astype(F32), tabs[w]) for w in range(3)],
        axis=0) for c in range(KV_COLS // LANES)]

    for g in range(N_KV_HEADS):
        kcol = kcols[g // 2]
        if g % 2 == 0:
            k_a = jnp.where(head_a, kcol, 0.0)
            k_b = pltpu.roll(k_a, 8, 1)
        else:
            k_b = jnp.where(head_a, 0.0, kcol)
            k_a = pltpu.roll(k_b, LANES - 8, 1)
        kst = jnp.concatenate([k_a, k_b], axis=0).astype(BF16)
        q2 = jnp.concatenate(
            [(_rope(q_ref[:, (2 * g + pr) * LANES:(2 * g + pr + 1) * LANES].astype(F32), tabs[1])
              * scale).astype(BF16) for pr in range(2)], axis=0)
        st = jax.lax.dot_general(kst, q2, (((1,), (1,)), ((), ())), preferred_element_type=F32)
        st = st + bias_ref[...]
        vg = v3[:, g * HEAD_DIM:(g + 1) * HEAD_DIM]
        for par in range(2):
            s = st[par * nkeys:(par + 1) * nkeys]
            h0, h1 = 4 * g + par, 4 * g + 2 + par
            sk = jnp.where(first_pair, sink_ref[h0], sink_ref[h1])
            m = jnp.maximum(jnp.max(s, axis=0, keepdims=True), sk)
            p = jnp.exp(s - m)
            denom = jnp.sum(p, axis=0, keepdims=True) + jnp.exp(sk - m)
            pn = (p * (1.0 / denom)).astype(BF16)
            o = jax.lax.dot_general(pn, vg, (((0,), (0,)), ((), ())), preferred_element_type=F32)
            acc_ref[:, h0 * HEAD_DIM:(h0 + 1) * HEAD_DIM] = o[:BLOCK]
            acc_ref[:, h1 * HEAD_DIM:(h1 + 1) * HEAD_DIM] = o[BLOCK:]

    a = acc_ref[...]
    an = a * jax.lax.rsqrt(jnp.mean(a * a, axis=-1, keepdims=True) + NORM_EPS) * g_ref[...]
    ga = ga_ref[...].astype(F32)
    o_ref[...] = (an * (ga * jax.nn.sigmoid(ga))).astype(o_ref.dtype)


def _attention(proj3, rope_tab, bias, sink, attn_g):
    B, L, _ = proj3.shape
    nb = L // BLOCK
    assert nb >= 2
    prev = lambda n: jnp.maximum(n - 1, 0)
    nxt = lambda n: jnp.minimum(n + 1, nb - 1)
    which = (prev, lambda n: n, nxt)

    def kv_spec(col, w):
        return pl.BlockSpec((None, BLOCK, KV_COLS), lambda b, n: (b, which[w](n), col))

    def tab_spec(w):
        return pl.BlockSpec((2, BLOCK, LANES), lambda b, n: (0, which[w](n), 0))

    return pl.pallas_call(
        _attn_kernel,
        grid=(B, nb),
        in_specs=[pl.BlockSpec(memory_space=pltpu.SMEM),
                  pl.BlockSpec((None, BLOCK, Q_COLS), lambda b, n: (b, n, COL_Q)),
                  pl.BlockSpec((None, BLOCK, D_ATTN), lambda b, n: (b, n, COL_GA)),
                  kv_spec(COL_K, 0), kv_spec(COL_K, 1), kv_spec(COL_K, 2),
                  kv_spec(COL_V, 0), kv_spec(COL_V, 1), kv_spec(COL_V, 2),
                  tab_spec(0), tab_spec(1), tab_spec(2),
                  pl.BlockSpec((None, 6 * BLOCK, 2 * BLOCK),
                               lambda b, n: (jnp.where(n == 0, 0, jnp.where(n == nb - 1, 2, 1)), 0, 0)),
                  pl.BlockSpec((1, D_ATTN), lambda b, n: (0, 0))],
        out_specs=pl.BlockSpec((None, BLOCK, D_ATTN), lambda b, n: (b, n, 0)),
        out_shape=jax.ShapeDtypeStruct((B, L, D_ATTN), BF16),
        scratch_shapes=[pltpu.VMEM((BLOCK, D_ATTN), F32)],
        compiler_params=_params("parallel", "parallel"),
        name="window_attention",
    )(sink, proj3, proj3, proj3, proj3, proj3, proj3, proj3, proj3,
      rope_tab, rope_tab, rope_tab, bias, attn_g.reshape(1, D_ATTN))


def _rope_table(L):
    inv = ROPE_THETA ** (-jnp.arange(0, ROT_DIM, 2, dtype=F32) / ROT_DIM)
    ang = jnp.arange(L, dtype=F32)[:, None] * inv[None, :]
    cos, sin = jnp.cos(ang), jnp.sin(ang)
    ones, zeros = jnp.ones((L, 8), F32), jnp.zeros((L, 8), F32)
    c_slots = [cos if s in (0, 4) else ones for s in range(8)]
    s_slots = [-sin if s == 0 else (sin if s == 4 else zeros) for s in range(8)]
    c = jnp.concatenate([x for s in range(8) for x in (c_slots[s], c_slots[s])], axis=1)
    s = jnp.concatenate([x for s in range(8) for x in (s_slots[s], s_slots[s])], axis=1)
    return jnp.stack([c, s])


def _attn_bias():
    c = np.arange(3 * BLOCK)[:, None]
    r = np.arange(BLOCK)[None, :]
    band = (c >= r) & (c <= r + 2 * WINDOW)
    variants = [band & (c >= BLOCK), band, band & (c < 2 * BLOCK)]
    out = np.stack([np.tile(np.where(v, 0.0, MASK_VALUE), (2, 2)) for v in variants])
    return jnp.asarray(out.astype(np.float32))


def _sconv_kernel(x_ref, w_ref, b_ref, o_ref, pad_ref, *, chunk):
    L, cb = x_ref.shape
    pad_ref[0:8, :] = jnp.zeros((8, cb), F32)
    pad_ref[L + 8:L + 16, :] = jnp.zeros((8, cb), F32)
    for r0 in range(0, L, chunk):
        pad_ref[8 + r0:8 + r0 + chunk, :] = x_ref[r0:r0 + chunk, :].astype(F32)
    w0, w1, w2, b = w_ref[0:1, :], w_ref[1:2, :], w_ref[2:3, :], b_ref[...]
    for r0 in range(0, L, chunk):
        xm = pad_ref[7 + r0:7 + r0 + chunk, :]
        x0 = pad_ref[8 + r0:8 + r0 + chunk, :]
        xp = pad_ref[9 + r0:9 + r0 + chunk, :]
        o_ref[r0:r0 + chunk, :] = xm * w0 + x0 * w1 + xp * w2 + b


def _short_conv(proj3, conv_w, conv_b, cb=256):
    B, L, _ = proj3.shape
    ncb = 3 * D_HYENA // cb
    off = COL_HV * 1024 // cb
    return pl.pallas_call(
        functools.partial(_sconv_kernel, chunk=min(L, 512)),
        grid=(B, ncb),
        in_specs=[pl.BlockSpec((None, L, cb), lambda b, c: (b, 0, off + c)),
                  pl.BlockSpec((3, cb), lambda b, c: (0, c)),
                  pl.BlockSpec((1, cb), lambda b, c: (0, c))],
        out_specs=pl.BlockSpec((None, L, cb), lambda b, c: (b, 0, c)),
        out_shape=jax.ShapeDtypeStruct((B, L, 3 * D_HYENA), F32),
        scratch_shapes=[pltpu.VMEM((L + 16, cb), F32)],
        compiler_params=_params("parallel", "parallel"),
        name="short_conv",
    )(proj3, conv_w, conv_b.reshape(1, -1))


def _split(a):
    hi = a.astype(BF16)
    return hi, (a - hi.astype(F32)).astype(BF16)


def _dot3(a, b):
    ah, al = _split(a)
    bh, bl = _split(b)
    d = lambda x, y: jnp.dot(x, y, preferred_element_type=F32)
    return d(ah, bh) + (d(ah, bl) + d(al, bh))


def _fmlp_kernel(bands_ref, w1t_ref, w1c_ref, w1s_ref, b1_ref, f1_ref, w2_ref, b2_ref, f2_ref,
                 w3f_ref, w3b_ref, df_ref, db_ref, of_ref, ob_ref, nrm_ref, *, L, tr):
    i = pl.program_id(0)
    pos = (i * tr + jax.lax.broadcasted_iota(jnp.int32, (tr, 1), 0)).astype(F32)

    def hidden(p):
        t = p / (L - 1)
        w = 2.0 * math.pi * p / L
        ang = w * bands_ref[...]
        pre = (t * w1t_ref[...] + _dot3(jnp.cos(ang), w1c_ref[...])
               + _dot3(-jnp.sin(ang), w1s_ref[...]) + b1_ref[...])
        h = jnp.sin(f1_ref[...] * pre)
        h = jnp.sin(f2_ref[...] * (_dot3(h, w2_ref[...]) + b2_ref[...]))
        return h, t

    h_a, t_a = hidden(pos)
    h_b, t_b = hidden(pos + 1.0)
    dec_b = jnp.abs(db_ref[...])
    out_f = _dot3(h_a, w3f_ref[...]) * jnp.exp(-t_a * jnp.abs(df_ref[...]))
    out_b = _dot3(h_b, w3b_ref[...]) * jnp.exp(-t_b * dec_b)
    out_b = jnp.where(pos + 1.0 <= L - 1, out_b, 0.0)
    of_ref[...] = out_f
    ob_ref[...] = out_b
    part = (jnp.sum(jnp.abs(out_f), axis=0, keepdims=True)
            + jnp.sum(jnp.abs(out_b), axis=0, keepdims=True))

    @pl.when(i == 0)
    def _():
        b0 = _dot3(h_a[0:8], w3b_ref[...]) * jnp.exp(-t_a[0:8] * dec_b)
        nrm_ref[...] = part + jnp.abs(b0[0:1])

    @pl.when(i > 0)
    def _():
        nrm_ref[...] += part


def _filter_mlp(L, w1, b1, f1, w2, b2, f2, w3, decay, tr=256):
    nf = 2 * D_HYENA
    w3r = w3.reshape(FILTER_HIDDEN, 2, 2, D_HYENA)
    dr = decay.reshape(2, 2, D_HYENA)
    w3f, w3b = w3r[:, :, 0].reshape(FILTER_HIDDEN, nf), w3r[:, :, 1].reshape(FILTER_HIDDEN, nf)
    df, db = dr[:, 0].reshape(1, nf), dr[:, 1].reshape(1, nf)
    bands = jnp.linspace(1e-4, FILTER_BANDS - 1, FILTER_BANDS, dtype=F32).reshape(1, FILTER_BANDS)
    H = FILTER_HIDDEN
    full = lambda shape: pl.BlockSpec(shape, lambda i: (0,) * len(shape))
    return pl.pallas_call(
        functools.partial(_fmlp_kernel, L=L, tr=tr),
        grid=(L // tr,),
        in_specs=[full((1, FILTER_BANDS)), full((1, H)), full((FILTER_BANDS, H)),
                  full((FILTER_BANDS, H)), full((1, H)), full((1, H)), full((H, H)),
                  full((1, H)), full((1, H)), full((H, nf)), full((H, nf)),
                  full((1, nf)), full((1, nf))],
        out_specs=[pl.BlockSpec((tr, nf), lambda i: (i, 0)),
                   pl.BlockSpec((tr, nf), lambda i: (i, 0)),
                   pl.BlockSpec((1, nf), lambda i: (0, 0))],
        out_shape=[jax.ShapeDtypeStruct((L, nf), F32), jax.ShapeDtypeStruct((L, nf), F32),
                   jax.ShapeDtypeStruct((1, nf), F32)],
        compiler_params=_params("arbitrary"),
        name="filter_mlp",
    )(bands, w1[0:1], w1[1:1 + FILTER_BANDS], w1[1 + FILTER_BANDS:], b1.reshape(1, H),
      f1.reshape(1, H), w2, b2.reshape(1, H), f2.reshape(1, H), w3f, w3b, df, db)


def _cs(num, den):
    ang = 2.0 * np.pi * (np.asarray(num, np.int64) % den).astype(np.float64) / den
    return np.cos(ang), np.sin(ang)


def _dft_constants(L, pairs):
    n = 2 * L
    n1 = 128 if L >= 8192 else 64
    n2 = n // n1
    n1h = n1 // 2
    q = n1h if pairs else n1 // 4
    ar = np.arange
    eye = np.eye(SUB)
    bf = lambda m: jnp.asarray(m.astype(np.float32)).astype(BF16)

    c, s = _cs(ar(n1)[:, None] * ar(q)[None, :], n1)
    f1 = np.stack([np.concatenate([c, s], axis=1), np.concatenate([-s, c], axis=1)], axis=1)
    k1 = np.kron(f1.reshape(2 * n1, 2 * q), eye)

    c, s = _cs(ar(n1)[:, None] * ar(n1h)[None, :], n1)
    k1f = np.kron(np.stack([c, -s], axis=1).reshape(2 * n1, n1h), eye)
    c, s = _cs(ar(n1)[:, None] * (n1 - 1 - ar(n1h))[None, :], n1)
    k1b = np.kron(np.stack([c, -s], axis=1).reshape(2 * n1, n1h), eye[::-1])

    c2, s2 = (jnp.asarray(x.astype(np.float32)) for x in _cs(ar(n2)[:, None] * ar(n2)[None, :], n2))
    ct, st = (jnp.asarray(x.astype(np.float32)) for x in _cs(ar(n1)[:, None] * ar(n2)[None, :], n))
    gr = c2[None] * ct[:, None, :] - s2[None] * st[:, None, :]
    gi = -(s2[None] * ct[:, None, :] + c2[None] * st[:, None, :])
    gs = jnp.concatenate([jnp.concatenate([gr, -gi], axis=2),
                          jnp.concatenate([gi, gr], axis=2)], axis=1).astype(BF16)
    gst = jnp.swapaxes(gs, 1, 2)

    c, s = _cs(ar(n1h)[:, None] * ar(n1)[None, :], n1)
    if pairs:
        top = np.stack([c, -s], axis=2).reshape(n1h, 2 * n1)
        bot = np.stack([s, c], axis=2).reshape(n1h, 2 * n1)
        f3 = np.concatenate([top, bot], axis=0)
    else:
        c2h, s2h = _cs((ar(n1h)[:, None] - q) * ar(n1)[None, :], n1)
        f3 = np.stack([c + s2h, -s + c2h], axis=2).reshape(n1h, 2 * n1)
    k3 = np.kron(f3 / n, eye)
    return dict(n1=n1, n2=n2, q=q, k1=bf(k1), k1f=bf(k1f), k1b=bf(k1b), gs=gs, gst=gst, k3=bf(k3))


def _halves(x):
    cb = x.shape[-1]
    return [x[..., h * SUB:(h + 1) * SUB, :].reshape(-1, cb).astype(BF16) for h in range(2)]


def _join(lo, hi, lead):
    cb = lo.shape[-1]
    return jnp.concatenate([lo.reshape(*lead, SUB, cb), hi.reshape(*lead, SUB, cb)], axis=len(lead))


def _inner_forward(scr_ref, k1_idx, gs):
    n2 = scr_ref.shape[2]
    a = scr_ref[k1_idx].reshape(2 * n2, scr_ref.shape[3])
    return jnp.dot(gs, a, preferred_element_type=F32)


def _conv_kernel(u_ref, g_ref, d_ref, k1_ref, k3_ref, gs_ref, gst_ref, kf_ref, o_ref, scr_ref,
                 *, s1, s2):
    n1, _, n2, cb = scr_ref.shape
    kb = gs_ref.shape[0]
    q = u_ref.shape[1]
    s = pl.program_id(2)

    @pl.when(s < s1)
    def _():
        lo, hi = [jnp.dot(k1_ref[...], x, preferred_element_type=F32) for x in _halves(u_ref[...])]
        row0 = pl.multiple_of(s * PACK, PACK)
        scr_ref[:, :, pl.ds(row0, PACK), :] = _join(lo, hi, (n1, 2)).astype(BF16)

    @pl.when((s >= s1) & (s < s1 + s2))
    def _():
        for kk in range(kb):
            k1_idx = (s - s1) * kb + kk
            u = _inner_forward(scr_ref, k1_idx, gs_ref[kk])
            ur, ui = u[:n2], u[n2:]
            kr, ki = kf_ref[kk, 0].astype(F32), kf_ref[kk, 1].astype(F32)
            vr = (ur * kr - ui * ki).astype(BF16)
            vi = (ur * ki + ui * kr).astype(BF16)
            gt = gst_ref[kk]
            b = (jnp.dot(gt[:, :n2], vr, preferred_element_type=F32)
                 + jnp.dot(gt[:, n2:], vi, preferred_element_type=F32))
            scr_ref[k1_idx] = b.reshape(2, n2, cb).astype(BF16)

    @pl.when(s >= s1 + s2)
    def _():
        row0 = pl.multiple_of((s - s1 - s2) * PACK, PACK)
        b = scr_ref[:, :, pl.ds(row0, PACK), :].astype(F32)
        lo, hi = [jnp.dot(k3_ref[...], x, preferred_element_type=F32) for x in _halves(b)]
        y = _join(lo, hi, (2, q))
        o_ref[...] = g_ref[...] * (y + u_ref[...] * d_ref[...])


def _long_conv(u6, u_off, g6, g_off, d, kf, order, dc, cb, kb):
    bp, _, q, n2p, _, _ = u6.shape
    C = D_HYENA
    n1, n2 = dc["n1"], dc["n2"]
    s1, s2 = n2p, n1 // kb
    koff = order * (C // cb)
    grp1 = lambda s: jnp.where(s < s1, s, jnp.where(s < s1 + s2, s1 - 1, s - s1 - s2))
    grp3 = lambda s: jnp.maximum(s - s1 - s2, 0)
    kblk = lambda s: jnp.clip(s - s1, 0, s2 - 1)
    tspec = lambda off, grp: pl.BlockSpec((None, 2, q, None, PACK, cb),
                                          lambda b, c, s: (b, 0, 0, grp(s), 0, off + c))
    const = lambda a: pl.BlockSpec(a.shape, lambda b, c, s: (0,) * a.ndim)
    return pl.pallas_call(
        functools.partial(_conv_kernel, s1=s1, s2=s2),
        grid=(bp, C // cb, s1 + s2 + s1),
        in_specs=[tspec(u_off, grp1), tspec(g_off, grp3),
                  pl.BlockSpec((1, cb), lambda b, c, s: (0, c)),
                  const(dc["k1"]), const(dc["k3"]),
                  pl.BlockSpec((kb, 2 * n2, 2 * n2), lambda b, c, s: (kblk(s), 0, 0)),
                  pl.BlockSpec((kb, 2 * n2, 2 * n2), lambda b, c, s: (kblk(s), 0, 0)),
                  pl.BlockSpec((kb, 2, n2, cb), lambda b, c, s: (kblk(s), 0, 0, koff + c))],
        out_specs=tspec(0, grp3),
        out_shape=jax.ShapeDtypeStruct((bp, 2, q, n2p, PACK, C), F32),
        scratch_shapes=[pltpu.VMEM((n1, 2, n2, cb), BF16)],
        compiler_params=_params("parallel", "parallel", "arbitrary"),
        name="long_conv",
    )(u6, g6, d.reshape(1, C), dc["k1"], dc["k3"], dc["gs"], dc["gst"], kf)


def _hyena(hyc, kf, dc, hy_d, cb, kb):
    B, L, _ = hyc.shape
    C = D_HYENA
    n2, q = dc["n2"], dc["q"]
    bp = B * L // (2 * q * n2)
    hy6 = hyc.reshape(bp, 2, q, n2 // PACK, PACK, 3 * C)
    z6 = _long_conv(hy6, 0, hy6, C // cb, hy_d[0], kf, 0, dc, cb, kb)
    z6 = _long_conv(z6, 0, hy6, 2 * (C // cb), hy_d[1], kf, 1, dc, cb, kb)
    return z6.reshape(B, L, C)


def _filt_kernel(xf_ref, xb_ref, nrm_ref, k1f_ref, k1b_ref, gs_ref, o_ref, scr_ref, *, s1):
    n1, _, n2, cb = scr_ref.shape
    kb = gs_ref.shape[0]
    s = pl.program_id(1)

    @pl.when(s < s1)
    def _():
        f_lo, f_hi = _halves(xf_ref[...])
        b_lo, b_hi = _halves(xb_ref[...])
        d = lambda k, x: jnp.dot(k[...], x, preferred_element_type=F32)
        lo = d(k1f_ref, f_lo) + d(k1b_ref, b_hi)
        hi = d(k1f_ref, f_hi) + d(k1b_ref, b_lo)
        row0 = pl.multiple_of(s * PACK, PACK)
        scr_ref[:, :, pl.ds(row0, PACK), :] = _join(lo, hi, (n1, 2)).astype(BF16)

    @pl.when(s >= s1)
    def _():
        inv = 1.0 / nrm_ref[...]
        for kk in range(kb):
            u = _inner_forward(scr_ref, (s - s1) * kb + kk, gs_ref[kk]) * inv
            o_ref[kk] = u.reshape(2, n2, cb).astype(o_ref.dtype)


def _filter_spectrum(L, dc, fw1, fb1, ff1, fw2, fb2, ff2, fw3, fdecay, cb, kb):
    hf, hb, nrm = _filter_mlp(L, fw1, fb1, ff1, fw2, fb2, ff2, fw3, fdecay)
    nf = hf.shape[1]
    n1, n2 = dc["n1"], dc["n2"]
    n1h, n2p = n1 // 2, n2 // PACK
    s1, s2 = n2p, n1 // kb
    hf4 = hf.reshape(n1h, n2p, PACK, nf)
    hb4 = hb.reshape(n1h, n2p, PACK, nf)
    grp = lambda s: jnp.minimum(s, s1 - 1)
    kblk = lambda s: jnp.maximum(s - s1, 0)
    const = lambda a: pl.BlockSpec(a.shape, lambda c, s: (0,) * a.ndim)
    return pl.pallas_call(
        functools.partial(_filt_kernel, s1=s1),
        grid=(nf // cb, s1 + s2),
        in_specs=[pl.BlockSpec((n1h, None, PACK, cb), lambda c, s: (0, grp(s), 0, c)),
                  pl.BlockSpec((n1h, None, PACK, cb), lambda c, s: (0, s1 - 1 - grp(s), 0, c)),
                  pl.BlockSpec((1, cb), lambda c, s: (0, c)),
                  const(dc["k1f"]), const(dc["k1b"]),
                  pl.BlockSpec((kb, 2 * n2, 2 * n2), lambda c, s: (kblk(s), 0, 0))],
        out_specs=pl.BlockSpec((kb, 2, n2, cb), lambda c, s: (kblk(s), 0, 0, c)),
        out_shape=jax.ShapeDtypeStruct((n1, 2, n2, nf), BF16),
        scratch_shapes=[pltpu.VMEM((n1, 2, n2, cb), BF16)],
        compiler_params=_params("parallel", "arbitrary"),
        name="filter_spectrum",
    )(hf4, hb4, nrm, dc["k1f"], dc["k1b"], dc["gs"])


def _out_kernel(a_ref, z_ref, gh_ref, hg_ref, wa_ref, wz_ref, h_ref, g_ref, b_ref, o_ref, ob_ref):
    z = z_ref[...]
    gh = gh_ref[...].astype(F32)
    zn = z * jax.lax.rsqrt(jnp.mean(z * z, axis=-1, keepdims=True) + NORM_EPS) * hg_ref[...]
    zn = (zn * (gh * jax.nn.sigmoid(gh))).astype(BF16)
    acc = (jnp.dot(a_ref[...], wa_ref[...], preferred_element_type=F32)
           + jnp.dot(zn, wz_ref[...], preferred_element_type=F32))
    y = DN_ALPHA * h_ref[...] + acc
    mu = jnp.mean(y, axis=-1, keepdims=True)
    yc = y - mu
    var = jnp.mean(yc * yc, axis=-1, keepdims=True)
    out = yc * jax.lax.rsqrt(var + NORM_EPS) * g_ref[...] + b_ref[...]
    o_ref[...] = out
    ob_ref[...] = out.astype(BF16)


def _out_proj(a, z, proj, hy_g, w_out, h, ln_g, ln_b, tm=256):
    T, D = h.shape
    wa = w_out[:D_ATTN].astype(BF16)
    wz = w_out[D_ATTN:].astype(BF16)
    row = lambda w: pl.BlockSpec((tm, w), lambda i: (i, 0))
    const = lambda r, w: pl.BlockSpec((r, w), lambda i: (0, 0))
    return pl.pallas_call(
        _out_kernel,
        grid=(T // tm,),
        in_specs=[row(D_ATTN), row(D_HYENA),
                  pl.BlockSpec((tm, D_HYENA), lambda i: (i, COL_GH)),
                  const(1, D_HYENA), const(D_ATTN, D), const(D_HYENA, D), row(D),
                  const(1, D), const(1, D)],
        out_specs=[row(D), row(D)],
        out_shape=[jax.ShapeDtypeStruct((T, D), F32), jax.ShapeDtypeStruct((T, D), BF16)],
        compiler_params=_params("parallel"),
        name="out_proj",
    )(a, z, proj, hy_g.reshape(1, -1), wa, wz, h, ln_g.reshape(1, D), ln_b.reshape(1, D))


def _arrange_w_in(w):
    q, k, v, ga, hy, gh = jnp.split(w, [1024, 1280, 1536, 2560, 5632], axis=1)
    pair = _head_pair_lane_perm()
    pq = np.concatenate([pair + LANES * i for i in range(Q_COLS // LANES)])
    pk = np.concatenate([pair + LANES * i for i in range(KV_COLS // LANES)])
    return jnp.concatenate([q[:, pq], ga, gh, hy, k[:, pk], v], axis=1).astype(BF16)


def _trunk(x, p):
    B, L, D = x.shape
    T = B * L
    pairs = B > 1
    dc = _dft_constants(L, pairs)
    cb, kb = (512, 16) if L <= 2048 else (256, 8)
    rope_tab = _rope_table(L)
    bias = _attn_bias()
    h, hb = _layernorm(x.reshape(T, D), p["emb_ln_g"], p["emb_ln_b"])
    for l in range(DEPTH):
        proj = _matmul(hb, _arrange_w_in(p["w_in"][l]), 1024, 512, BF16)
        proj3 = proj.reshape(B, L, D_IN)
        a = _attention(proj3, rope_tab, bias, p["attn_sink"][l], p["attn_norm_g"][l])
        hyc = _short_conv(proj3, p["conv_w"][l], p["conv_b"][l])
        kf = _filter_spectrum(L, dc, p["flt_w1"][l], p["flt_b1"][l], p["flt_freq1"][l],
                              p["flt_w2"][l], p["flt_b2"][l], p["flt_freq2"][l],
                              p["flt_w3"][l], p["flt_decay"][l], cb, kb)
        z = _hyena(hyc, kf, dc, p["hyena_d"][l], cb, kb)
        h, hb = _out_proj(a.reshape(T, D_ATTN), z.reshape(T, D_HYENA), proj,
                          p["hyena_norm_g"][l], p["w_out"][l], h, p["ln_g"][l], p["ln_b"][l])
    return h.reshape(B, L, D)


def kernel(x_prompt, x_sample, emb_ln_g, emb_ln_b, w_in, attn_sink, conv_w, conv_b, flt_w1,
           flt_b1, flt_freq1, flt_w2, flt_b2, flt_freq2, flt_w3, flt_decay, hyena_d,
           attn_norm_g, hyena_norm_g, w_out, ln_g, ln_b):
    p = dict(emb_ln_g=emb_ln_g, emb_ln_b=emb_ln_b, w_in=w_in, attn_sink=attn_sink,
             conv_w=conv_w, conv_b=conv_b, flt_w1=flt_w1, flt_b1=flt_b1, flt_freq1=flt_freq1,
             flt_w2=flt_w2, flt_b2=flt_b2, flt_freq2=flt_freq2, flt_w3=flt_w3,
             flt_decay=flt_decay, hyena_d=hyena_d, attn_norm_g=attn_norm_g,
             hyena_norm_g=hyena_norm_g, w_out=w_out, ln_g=ln_g, ln_b=ln_b)
    return (_trunk(x_prompt, p), _trunk(x_sample, p))
```

```python
import functools
import math

import numpy as np
import jax
import jax.numpy as jnp
from jax.experimental import pallas as pl
from jax.experimental.pallas import tpu as pltpu

F32 = jnp.float32
BF16 = jnp.bfloat16

D_MODEL = 2048
DEPTH = 2
D_ATTN = 1024
D_HYENA = 1024
HEAD_DIM = 64
N_HEADS = 16
N_KV_HEADS = 4
GQA_GROUPS = 4
ROT_DIM = 16
ROPE_THETA = 500000.0
WINDOW = 128
BLOCK = 128
FILTER_BANDS = 16
FILTER_HIDDEN = 64
DN_ALPHA = (2.0 * DEPTH) ** 0.25
NORM_EPS = 1e-5
MASK_VALUE = -1e30
Q_COLS = N_HEADS * HEAD_DIM
KV_COLS = N_KV_HEADS * HEAD_DIM
D_IN = 2 * Q_COLS + 2 * KV_COLS + 4 * D_HYENA

COL_Q, COL_GA, COL_GH, COL_HV, COL_HX1, COL_HX2 = 0, 1, 2, 3, 4, 5
COL_K, COL_V = 24, 25

LANES = 128
SUB = 8
PACK = 16
VMEM_LIMIT = 48 * 1024 * 1024


def _params(*sem):
    return pltpu.CompilerParams(dimension_semantics=sem, vmem_limit_bytes=VMEM_LIMIT)


def _ln_kernel(x_ref, g_ref, b_ref, o_ref, ob_ref):
    x = x_ref[...]
    mu = jnp.mean(x, axis=-1, keepdims=True)
    xc = x - mu
    var = jnp.mean(xc * xc, axis=-1, keepdims=True)
    y = xc * jax.lax.rsqrt(var + NORM_EPS) * g_ref[...] + b_ref[...]
    o_ref[...] = y
    ob_ref[...] = y.astype(BF16)


def _layernorm(x, g, b, tm=512):
    T, D = x.shape
    return pl.pallas_call(
        _ln_kernel,
        grid=(T // tm,),
        in_specs=[pl.BlockSpec((tm, D), lambda i: (i, 0)),
                  pl.BlockSpec((1, D), lambda i: (0, 0)),
                  pl.BlockSpec((1, D), lambda i: (0, 0))],
        out_specs=[pl.BlockSpec((tm, D), lambda i: (i, 0)),
                   pl.BlockSpec((tm, D), lambda i: (i, 0))],
        out_shape=[jax.ShapeDtypeStruct((T, D), F32), jax.ShapeDtypeStruct((T, D), BF16)],
        compiler_params=_params("parallel"),
        name="layernorm",
    )(x, g.reshape(1, D), b.reshape(1, D))


def _mm_kernel(x_ref, w_ref, o_ref):
    o_ref[...] = jnp.dot(x_ref[...], w_ref[...], preferred_element_type=F32).astype(o_ref.dtype)


def _matmul(x, w, tm, tn, out_dtype):
    M, K = x.shape
    N = w.shape[1]
    return pl.pallas_call(
        _mm_kernel,
        grid=(M // tm, N // tn),
        in_specs=[pl.BlockSpec((tm, K), lambda i, j: (i, 0)),
                  pl.BlockSpec((K, tn), lambda i, j: (0, j))],
        out_specs=pl.BlockSpec((tm, tn), lambda i, j: (i, j)),
        out_shape=jax.ShapeDtypeStruct((M, N), out_dtype),
        compiler_params=_params("parallel", "parallel"),
        name="in_proj",
    )(x, w)


_SLOT_OF_OCTET = (0, 4, 1, 2, 3, 5, 6, 7)


def _head_pair_lane_perm():
    perm = np.zeros(LANES, np.int32)
    for octet, slot in enumerate(_SLOT_OF_OCTET):
        for hd in range(2):
            for i in range(8):
                perm[16 * slot + 8 * hd + i] = hd * HEAD_DIM + 8 * octet + i
    return perm


def _rope(t, tab):
    return t * tab[0] + pltpu.roll(t, LANES // 2, 1) * tab[1]


def _attn_kernel(sink_ref, q_ref, ga_ref, kp_ref, kc_ref, kn_ref, vp_ref, vc_ref, vn_ref,
                 tp_ref, tc_ref, tn_ref, bias_ref, g_ref, o_ref, acc_ref):
    nkeys = 3 * BLOCK
    tabs = (tp_ref[...], tc_ref[...], tn_ref[...])
    krefs = (kp_ref, kc_ref, kn_ref)
    v3 = jnp.concatenate([vp_ref[...], vc_ref[...], vn_ref[...]], axis=0)
    lane = jax.lax.broadcasted_iota(jnp.int32, (1, LANES), 1)
    head_a = (lane % 16) < 8
    first_pair = jax.lax.broadcasted_iota(jnp.int32, (1, 2 * BLOCK), 1) < BLOCK
    scale = HEAD_DIM ** -0.5
    kcols = [jnp.concatenate(
        [_rope(krefs[w][:, c * LANES:(c + 1) * LANES].astype(F32), tabs[w]) for w in range(3)],
        axis=0) for c in range(KV_COLS // LANES)]

    for g in range(N_KV_HEADS):
        kcol = kcols[g // 2]
        if g % 2 == 0:
            k_a = jnp.where(head_a, kcol, 0.0)
            k_b = pltpu.roll(k_a, 8, 1)
        else:
            k_b = jnp.where(head_a, 0.0, kcol)
            k_a = pltpu.roll(k_b, LANES - 8, 1)
        kst = jnp.concatenate([k_a, k_b], axis=0).astype(BF16)
        q2 = jnp.concatenate(
            [(_rope(q_ref[:, (2 * g + pr) * LANES:(2 * g + pr + 1) * LANES].astype(F32), tabs[1])
              * scale).astype(BF16) for pr in range(2)], axis=0)
        st = jax.lax.dot_general(kst, q2, (((1,), (1,)), ((), ())), preferred_element_type=F32)
        st = st + bias_ref[...]
        vg = v3[:, g * HEAD_DIM:(g + 1) * HEAD_DIM]
        for par in range(2):
            s = st[par * nkeys:(par + 1) * nkeys]
            h0, h1 = 4 * g + par, 4 * g + 2 + par
            sk = jnp.where(first_pair, sink_ref[h0], sink_ref[h1])
            m = jnp.maximum(jnp.max(s, axis=0, keepdims=True), sk)
            p = jnp.exp(s - m)
            denom = jnp.sum(p, axis=0, keepdims=True) + jnp.exp(sk - m)
            pn = (p * (1.0 / denom)).astype(BF16)
            o = jax.lax.dot_general(pn, vg, (((0,), (0,)), ((), ())), preferred_element_type=F32)
            acc_ref[:, h0 * HEAD_DIM:(h0 + 1) * HEAD_DIM] = o[:BLOCK]
            acc_ref[:, h1 * HEAD_DIM:(h1 + 1) * HEAD_DIM] = o[BLOCK:]

    a = acc_ref[...]
    an = a * jax.lax.rsqrt(jnp.mean(a * a, axis=-1, keepdims=True) + NORM_EPS) * g_ref[...]
    ga = ga_ref[...].astype(F32)
    o_ref[...] = (an * (ga * jax.nn.sigmoid(ga))).astype(o_ref.dtype)


def _attention(proj3, rope_tab, bias, sink, attn_g):
    B, L, _ = proj3.shape
    nb = L // BLOCK
    assert nb >= 2
    prev = lambda n: jnp.maximum(n - 1, 0)
    nxt = lambda n: jnp.minimum(n + 1, nb - 1)
    which = (prev, lambda n: n, nxt)

    def kv_spec(col, w):
        return pl.BlockSpec((None, BLOCK, KV_COLS), lambda b, n: (b, which[w](n), col))

    def tab_spec(w):
        return pl.BlockSpec((2, BLOCK, LANES), lambda b, n: (0, which[w](n), 0))

    return pl.pallas_call(
        _attn_kernel,
        grid=(B, nb),
        in_specs=[pl.BlockSpec(memory_space=pltpu.SMEM),
                  pl.BlockSpec((None, BLOCK, Q_COLS), lambda b, n: (b, n, COL_Q)),
                  pl.BlockSpec((None, BLOCK, D_ATTN), lambda b, n: (b, n, COL_GA)),
                  kv_spec(COL_K, 0), kv_spec(COL_K, 1), kv_spec(COL_K, 2),
                  kv_spec(COL_V, 0), kv_spec(COL_V, 1), kv_spec(COL_V, 2),
                  tab_spec(0), tab_spec(1), tab_spec(2),
                  pl.BlockSpec((None, 6 * BLOCK, 2 * BLOCK),
                               lambda b, n: (jnp.where(n == 0, 0, jnp.where(n == nb - 1, 2, 1)), 0, 0)),
                  pl.BlockSpec((1, D_ATTN), lambda b, n: (0, 0))],
        out_specs=pl.BlockSpec((None, BLOCK, D_ATTN), lambda b, n: (b, n, 0)),
        out_shape=jax.ShapeDtypeStruct((B, L, D_ATTN), BF16),
        scratch_shapes=[pltpu.VMEM((BLOCK, D_ATTN), F32)],
        compiler_params=_params("parallel", "parallel"),
        name="window_attention",
    )(sink, proj3, proj3, proj3, proj3, proj3, proj3, proj3, proj3,
      rope_tab, rope_tab, rope_tab, bias, attn_g.reshape(1, D_ATTN))


def _rope_table(L):
    inv = ROPE_THETA ** (-jnp.arange(0, ROT_DIM, 2, dtype=F32) / ROT_DIM)
    ang = jnp.arange(L, dtype=F32)[:, None] * inv[None, :]
    cos, sin = jnp.cos(ang), jnp.sin(ang)
    ones, zeros = jnp.ones((L, 8), F32), jnp.zeros((L, 8), F32)
    c_slots = [cos if s in (0, 4) else ones for s in range(8)]
    s_slots = [-sin if s == 0 else (sin if s == 4 else zeros) for s in range(8)]
    c = jnp.concatenate([x for s in range(8) for x in (c_slots[s], c_slots[s])], axis=1)
    s = jnp.concatenate([x for s in range(8) for x in (s_slots[s], s_slots[s])], axis=1)
    return jnp.stack([c, s])


def _attn_bias():
    c = np.arange(3 * BLOCK)[:, None]
    r = np.arange(BLOCK)[None, :]
    band = (c >= r) & (c <= r + 2 * WINDOW)
    variants = [band & (c >= BLOCK), band, band & (c < 2 * BLOCK)]
    out = np.stack([np.tile(np.where(v, 0.0, MASK_VALUE), (2, 2)) for v in variants])
    return jnp.asarray(out.astype(np.float32))


def _sconv_kernel(x_ref, w_ref, b_ref, o_ref, pad_ref, *, chunk):
    L, cb = x_ref.shape
    pad_ref[0:8, :] = jnp.zeros((8, cb), F32)
    pad_ref[L + 8:L + 16, :] = jnp.zeros((8, cb), F32)
    for r0 in range(0, L, chunk):
        pad_ref[8 + r0:8 + r0 + chunk, :] = x_ref[r0:r0 + chunk, :].astype(F32)
    w0, w1, w2, b = w_ref[0:1, :], w_ref[1:2, :], w_ref[2:3, :], b_ref[...]
    for r0 in range(0, L, chunk):
        xm = pad_ref[7 + r0:7 + r0 + chunk, :]
        x0 = pad_ref[8 + r0:8 + r0 + chunk, :]
        xp = pad_ref[9 + r0:9 + r0 + chunk, :]
        o_ref[r0:r0 + chunk, :] = (xm * w0 + x0 * w1 + xp * w2 + b).astype(o_ref.dtype)


def _short_conv(proj3, conv_w, conv_b, cb=256):
    B, L, _ = proj3.shape
    ncb = 3 * D_HYENA // cb
    off = COL_HV * 1024 // cb
    return pl.pallas_call(
        functools.partial(_sconv_kernel, chunk=min(L, 512)),
        grid=(B, ncb),
        in_specs=[pl.BlockSpec((None, L, cb), lambda b, c: (b, 0, off + c)),
                  pl.BlockSpec((3, cb), lambda b, c: (0, c)),
                  pl.BlockSpec((1, cb), lambda b, c: (0, c))],
        out_specs=pl.BlockSpec((None, L, cb), lambda b, c: (b, 0, c)),
        out_shape=jax.ShapeDtypeStruct((B, L, 3 * D_HYENA), BF16),
        scratch_shapes=[pltpu.VMEM((L + 16, cb), F32)],
        compiler_params=_params("parallel", "parallel"),
        name="short_conv",
    )(proj3, conv_w, conv_b.reshape(1, -1))


def _split(a):
    hi = a.astype(BF16)
    return hi, (a - hi.astype(F32)).astype(BF16)


def _dot3(a, b):
    ah, al = _split(a)
    bh, bl = _split(b)
    d = lambda x, y: jnp.dot(x, y, preferred_element_type=F32)
    return d(ah, bh) + (d(ah, bl) + d(al, bh))


def _dot3_tn(at, b):
    ah, al = _split(at)
    bh, bl = _split(b)
    d = lambda x, y: jax.lax.dot_general(x, y, (((0,), (0,)), ((), ())), preferred_element_type=F32)
    return d(ah, bh) + (d(ah, bl) + d(al, bh))


def _fmlp_kernel(bands_ref, w1t_ref, w1c_ref, w1s_ref, b1_ref, f1_ref, w2_ref, b2_ref, f2_ref,
                 w3f_ref, w3b_ref, df_ref, db_ref, of_ref, ob_ref, nrm_ref, *, L, tr):
    i = pl.program_id(0)
    n = tr + LANES
    lag = (i * tr + jax.lax.broadcasted_iota(jnp.int32, (1, n), 1)).astype(F32)
    t_row = lag / (L - 1)
    ang = bands_ref[...] * (2.0 * math.pi * lag / L)
    pre = (w1t_ref[...] * t_row + _dot3(w1c_ref[...], jnp.cos(ang))
           + _dot3(w1s_ref[...], -jnp.sin(ang)) + b1_ref[...])
    h = jnp.sin(f1_ref[...] * pre)
    h = jnp.sin(f2_ref[...] * (_dot3(w2_ref[...], h) + b2_ref[...]))

    h_a = h[:, :tr]
    h_b = pltpu.roll(h, n - 1, 1)[:, :tr]
    pos = (i * tr + jax.lax.broadcasted_iota(jnp.int32, (tr, 1), 0)).astype(F32)
    t_a = pos / (L - 1)
    t_b = (pos + 1.0) / (L - 1)
    dec_b = jnp.abs(db_ref[...])
    out_f = _dot3_tn(h_a, w3f_ref[...]) * jnp.exp(-t_a * jnp.abs(df_ref[...]))
    out_b = _dot3_tn(h_b, w3b_ref[...]) * jnp.exp(-t_b * dec_b)
    out_b = jnp.where(pos + 1.0 <= L - 1, out_b, 0.0)
    of_ref[...] = out_f
    ob_ref[...] = out_b
    part = (jnp.sum(jnp.abs(out_f), axis=0, keepdims=True)
            + jnp.sum(jnp.abs(out_b), axis=0, keepdims=True))

    @pl.when(i == 0)
    def _():
        b0 = _dot3_tn(h[:, :LANES], w3b_ref[...])
        nrm_ref[...] = part + jnp.abs(b0[0:1])

    @pl.when(i > 0)
    def _():
        nrm_ref[...] += part


def _filter_mlp(L, w1, b1, f1, w2, b2, f2, w3, decay, tr=256):
    nf = 2 * D_HYENA
    w3r = w3.reshape(FILTER_HIDDEN, 2, 2, D_HYENA)
    dr = decay.reshape(2, 2, D_HYENA)
    w3f, w3b = w3r[:, :, 0].reshape(FILTER_HIDDEN, nf), w3r[:, :, 1].reshape(FILTER_HIDDEN, nf)
    df, db = dr[:, 0].reshape(1, nf), dr[:, 1].reshape(1, nf)
    bands = jnp.linspace(1e-4, FILTER_BANDS - 1, FILTER_BANDS, dtype=F32).reshape(FILTER_BANDS, 1)
    H = FILTER_HIDDEN
    col = lambda v: v.reshape(H, 1)
    full = lambda shape: pl.BlockSpec(shape, lambda i: (0,) * len(shape))
    return pl.pallas_call(
        functools.partial(_fmlp_kernel, L=L, tr=tr),
        grid=(L // tr,),
        in_specs=[full((FILTER_BANDS, 1)), full((H, 1)), full((H, FILTER_BANDS)),
                  full((H, FILTER_BANDS)), full((H, 1)), full((H, 1)), full((H, H)),
                  full((H, 1)), full((H, 1)), full((H, nf)), full((H, nf)),
                  full((1, nf)), full((1, nf))],
        out_specs=[pl.BlockSpec((tr, nf), lambda i: (i, 0)),
                   pl.BlockSpec((tr, nf), lambda i: (i, 0)),
                   pl.BlockSpec((1, nf), lambda i: (0, 0))],
        out_shape=[jax.ShapeDtypeStruct((L, nf), F32), jax.ShapeDtypeStruct((L, nf), F32),
                   jax.ShapeDtypeStruct((1, nf), F32)],
        compiler_params=_params("arbitrary"),
        name="filter_mlp",
    )(bands, w1[0:1].T, w1[1:1 + FILTER_BANDS].T, w1[1 + FILTER_BANDS:].T, col(b1), col(f1),
      w2.T, col(b2), col(f2), w3f, w3b, df, db)


def _cs(num, den):
    ang = 2.0 * np.pi * (np.asarray(num, np.int64) % den).astype(np.float64) / den
    return np.cos(ang), np.sin(ang)


def _dft_constants(L, pairs):
    n = 2 * L
    n1 = 128 if L >= 8192 else 64
    n2 = n // n1
    n1h = n1 // 2
    q = n1h if pairs else n1 // 4
    ar = np.arange
    eye = np.eye(SUB)
    bf = lambda m: jnp.asarray(m.astype(np.float32)).astype(BF16)

    c, s = _cs(ar(n1)[:, None] * ar(q)[None, :], n1)
    f1 = np.stack([np.concatenate([c, s], axis=1), np.concatenate([-s, c], axis=1)], axis=1)
    k1 = np.kron(f1.reshape(2 * n1, 2 * q), eye)

    c, s = _cs(ar(n1)[:, None] * ar(n1h)[None, :], n1)
    k1f = np.kron(np.stack([c, -s], axis=1).reshape(2 * n1, n1h), eye)
    c, s = _cs(ar(n1)[:, None] * (n1 - 1 - ar(n1h))[None, :], n1)
    k1b = np.kron(np.stack([c, -s], axis=1).reshape(2 * n1, n1h), eye[::-1])

    c2, s2 = (jnp.asarray(x.astype(np.float32)) for x in _cs(ar(n2)[:, None] * ar(n2)[None, :], n2))
    ct, st = (jnp.asarray(x.astype(np.float32)) for x in _cs(ar(n1)[:, None] * ar(n2)[None, :], n))
    gr = c2[None] * ct[:, None, :] - s2[None] * st[:, None, :]
    gi = -(s2[None] * ct[:, None, :] + c2[None] * st[:, None, :])
    gs = jnp.concatenate([jnp.concatenate([gr, -gi], axis=2),
                          jnp.concatenate([gi, gr], axis=2)], axis=1).astype(BF16)
    gst = jnp.swapaxes(gs, 1, 2)

    c, s = _cs(ar(n1h)[:, None] * ar(n1)[None, :], n1)
    if pairs:
        top = np.stack([c, -s], axis=2).reshape(n1h, 2 * n1)
        bot = np.stack([s, c], axis=2).reshape(n1h, 2 * n1)
        f3 = np.concatenate([top, bot], axis=0)
    else:
        c2h, s2h = _cs((ar(n1h)[:, None] - q) * ar(n1)[None, :], n1)
        f3 = np.stack([c + s2h, -s + c2h], axis=2).reshape(n1h, 2 * n1)
    k3 = np.kron(f3 / n, eye)
    return dict(n1=n1, n2=n2, q=q, k1=bf(k1), k1f=bf(k1f), k1b=bf(k1b), gs=gs, gst=gst, k3=bf(k3))


def _halves(x):
    cb = x.shape[-1]
    return [x[..., h * SUB:(h + 1) * SUB, :].reshape(-1, cb).astype(BF16) for h in range(2)]


def _join(lo, hi, lead):
    cb = lo.shape[-1]
    return jnp.concatenate([lo.reshape(*lead, SUB, cb), hi.reshape(*lead, SUB, cb)], axis=len(lead))


INNER_LANES = 256


def _inner_forward(scr_ref, k1_idx, c0, gs):
    n2 = scr_ref.shape[2]
    a = scr_ref[k1_idx, :, :, c0:c0 + INNER_LANES].reshape(2 * n2, INNER_LANES)
    return jnp.dot(gs, a, preferred_element_type=F32)


def _conv_kernel(u_ref, g_ref, d_ref, k1_ref, k3_ref, gs_ref, gst_ref, kf_ref, o_ref, scr_ref,
                 *, s1, s2):
    n1, _, n2, cb = scr_ref.shape
    kb = gs_ref.shape[0]
    q = u_ref.shape[1]
    s = pl.program_id(2)

    @pl.when(s < s1)
    def _():
        lo, hi = [jnp.dot(k1_ref[...], x, preferred_element_type=F32)
                  for x in _halves(u_ref[...].astype(F32))]
        row0 = pl.multiple_of(s * PACK, PACK)
        scr_ref[:, :, pl.ds(row0, PACK), :] = _join(lo, hi, (n1, 2)).astype(BF16)

    @pl.when((s >= s1) & (s < s1 + s2))
    def _():
        for kk in range(kb):
            k1_idx = (s - s1) * kb + kk
            for c0 in range(0, cb, INNER_LANES):
                u = _inner_forward(scr_ref, k1_idx, c0, gs_ref[kk])
                ur, ui = u[:n2], u[n2:]
                kr = kf_ref[kk, 0, :, c0:c0 + INNER_LANES].astype(F32)
                ki = kf_ref[kk, 1, :, c0:c0 + INNER_LANES].astype(F32)
                v = jnp.concatenate([ur * kr - ui * ki, ur * ki + ui * kr], axis=0).astype(BF16)
                b = jnp.dot(gst_ref[kk], v, preferred_element_type=F32)
                scr_ref[k1_idx, :, :, c0:c0 + INNER_LANES] = (
                    b.reshape(2, n2, INNER_LANES).astype(BF16))

    @pl.when(s >= s1 + s2)
    def _():
        row0 = pl.multiple_of((s - s1 - s2) * PACK, PACK)
        b = scr_ref[:, :, pl.ds(row0, PACK), :].astype(F32)
        lo, hi = [jnp.dot(k3_ref[...], x, preferred_element_type=F32) for x in _halves(b)]
        y = _join(lo, hi, (2, q))
        z = g_ref[...].astype(F32) * (y + u_ref[...].astype(F32) * d_ref[...])
        o_ref[...] = z.astype(o_ref.dtype)


def _long_conv(u6, u_off, g6, g_off, d, kf, order, dc, cb, kb):
    bp, _, q, n2p, _, _ = u6.shape
    C = D_HYENA
    n1, n2 = dc["n1"], dc["n2"]
    s1, s2 = n2p, n1 // kb
    koff = order * (C // cb)
    grp1 = lambda s: jnp.where(s < s1, s, jnp.where(s < s1 + s2, s1 - 1, s - s1 - s2))
    grp3 = lambda s: jnp.maximum(s - s1 - s2, 0)
    kblk = lambda s: jnp.clip(s - s1, 0, s2 - 1)
    tspec = lambda off, grp: pl.BlockSpec((None, 2, q, None, PACK, cb),
                                          lambda b, c, s: (b, 0, 0, grp(s), 0, off + c))
    const = lambda a: pl.BlockSpec(a.shape, lambda b, c, s: (0,) * a.ndim)
    return pl.pallas_call(
        functools.partial(_conv_kernel, s1=s1, s2=s2),
        grid=(bp, C // cb, s1 + s2 + s1),
        in_specs=[tspec(u_off, grp1), tspec(g_off, grp3),
                  pl.BlockSpec((1, cb), lambda b, c, s: (0, c)),
                  const(dc["k1"]), const(dc["k3"]),
                  pl.BlockSpec((kb, 2 * n2, 2 * n2), lambda b, c, s: (kblk(s), 0, 0)),
                  pl.BlockSpec((kb, 2 * n2, 2 * n2), lambda b, c, s: (kblk(s), 0, 0)),
                  pl.BlockSpec((kb, 2, n2, cb), lambda b, c, s: (kblk(s), 0, 0, koff + c))],
        out_specs=tspec(0, grp3),
        out_shape=jax.ShapeDtypeStruct((bp, 2, q, n2p, PACK, C), BF16),
        scratch_shapes=[pltpu.VMEM((n1, 2, n2, cb), BF16)],
        compiler_params=_params("parallel", "parallel", "arbitrary"),
        name="long_conv",
    )(u6, g6, d.reshape(1, C), dc["k1"], dc["k3"], dc["gs"], dc["gst"], kf)


def _hyena(hyc, kf, dc, hy_d, cb, kb):
    B, L, _ = hyc.shape
    C = D_HYENA
    n2, q = dc["n2"], dc["q"]
    bp = B * L // (2 * q * n2)
    hy6 = hyc.reshape(bp, 2, q, n2 // PACK, PACK, 3 * C)
    z6 = _long_conv(hy6, 0, hy6, C // cb, hy_d[0], kf, 0, dc, cb, kb)
    z6 = _long_conv(z6, 0, hy6, 2 * (C // cb), hy_d[1], kf, 1, dc, cb, kb)
    return z6.reshape(B, L, C)


def _filt_kernel(xf_ref, xb_ref, nrm_ref, k1f_ref, k1b_ref, gs_ref, o_ref, scr_ref, *, s1):
    n1, _, n2, cb = scr_ref.shape
    kb = gs_ref.shape[0]
    s = pl.program_id(1)

    @pl.when(s < s1)
    def _():
        f_lo, f_hi = _halves(xf_ref[...])
        b_lo, b_hi = _halves(xb_ref[...])
        d = lambda k, x: jnp.dot(k[...], x, preferred_element_type=F32)
        lo = d(k1f_ref, f_lo) + d(k1b_ref, b_hi)
        hi = d(k1f_ref, f_hi) + d(k1b_ref, b_lo)
        row0 = pl.multiple_of(s * PACK, PACK)
        scr_ref[:, :, pl.ds(row0, PACK), :] = _join(lo, hi, (n1, 2)).astype(BF16)

    @pl.when(s >= s1)
    def _():
        for kk in range(kb):
            for c0 in range(0, cb, INNER_LANES):
                inv = 1.0 / nrm_ref[:, c0:c0 + INNER_LANES]
                u = _inner_forward(scr_ref, (s - s1) * kb + kk, c0, gs_ref[kk]) * inv
                o_ref[kk, :, :, c0:c0 + INNER_LANES] = (
                    u.reshape(2, n2, INNER_LANES).astype(o_ref.dtype))


def _filter_spectrum(L, dc, fw1, fb1, ff1, fw2, fb2, ff2, fw3, fdecay, cb, kb):
    hf, hb, nrm = _filter_mlp(L, fw1, fb1, ff1, fw2, fb2, ff2, fw3, fdecay)
    nf = hf.shape[1]
    n1, n2 = dc["n1"], dc["n2"]
    n1h, n2p = n1 // 2, n2 // PACK
    s1, s2 = n2p, n1 // kb
    hf4 = hf.reshape(n1h, n2p, PACK, nf)
    hb4 = hb.reshape(n1h, n2p, PACK, nf)
    grp = lambda s: jnp.minimum(s, s1 - 1)
    kblk = lambda s: jnp.maximum(s - s1, 0)
    const = lambda a: pl.BlockSpec(a.shape, lambda c, s: (0,) * a.ndim)
    return pl.pallas_call(
        functools.partial(_filt_kernel, s1=s1),
        grid=(nf // cb, s1 + s2),
        in_specs=[pl.BlockSpec((n1h, None, PACK, cb), lambda c, s: (0, grp(s), 0, c)),
                  pl.BlockSpec((n1h, None, PACK, cb), lambda c, s: (0, s1 - 1 - grp(s), 0, c)),
                  pl.BlockSpec((1, cb), lambda c, s: (0, c)),
                  const(dc["k1f"]), const(dc["k1b"]),
                  pl.BlockSpec((kb, 2 * n2, 2 * n2), lambda c, s: (kblk(s), 0, 0))],
        out_specs=pl.BlockSpec((kb, 2, n2, cb), lambda c, s: (kblk(s), 0, 0, c)),
        out_shape=jax.ShapeDtypeStruct((n1, 2, n2, nf), BF16),
        scratch_shapes=[pltpu.VMEM((n1, 2, n2, cb), BF16)],
        compiler_params=_params("parallel", "arbitrary"),
        name="filter_spectrum",
    )(hf4, hb4, nrm, dc["k1f"], dc["k1b"], dc["gs"])


def _out_kernel(a_ref, z_ref, gh_ref, hg_ref, wa_ref, wz_ref, h_ref, g_ref, b_ref, o_ref, ob_ref):
    z = z_ref[...].astype(F32)
    gh = gh_ref[...].astype(F32)
    zn = z * jax.lax.rsqrt(jnp.mean(z * z, axis=-1, keepdims=True) + NORM_EPS) * hg_ref[...]
    zn = (zn * (gh * jax.nn.sigmoid(gh))).astype(BF16)
    acc = (jnp.dot(a_ref[...], wa_ref[...], preferred_element_type=F32)
           + jnp.dot(zn, wz_ref[...], preferred_element_type=F32))
    y = DN_ALPHA * h_ref[...] + acc
    mu = jnp.mean(y, axis=-1, keepdims=True)
    yc = y - mu
    var = jnp.mean(yc * yc, axis=-1, keepdims=True)
    out = yc * jax.lax.rsqrt(var + NORM_EPS) * g_ref[...] + b_ref[...]
    o_ref[...] = out
    ob_ref[...] = out.astype(BF16)


def _out_proj(a, z, proj, hy_g, w_out, h, ln_g, ln_b, tm=512):
    T, D = h.shape
    wa = w_out[:D_ATTN].astype(BF16)
    wz = w_out[D_ATTN:].astype(BF16)
    row = lambda w: pl.BlockSpec((tm, w), lambda i: (i, 0))
    const = lambda r, w: pl.BlockSpec((r, w), lambda i: (0, 0), pipeline_mode=pl.Buffered(1))
    return pl.pallas_call(
        _out_kernel,
        grid=(T // tm,),
        in_specs=[row(D_ATTN), row(D_HYENA),
                  pl.BlockSpec((tm, D_HYENA), lambda i: (i, COL_GH)),
                  const(1, D_HYENA), const(D_ATTN, D), const(D_HYENA, D), row(D),
                  const(1, D), const(1, D)],
        out_specs=[row(D), row(D)],
        out_shape=[jax.ShapeDtypeStruct((T, D), F32), jax.ShapeDtypeStruct((T, D), BF16)],
        compiler_params=_params("parallel"),
        name="out_proj",
    )(a, z, proj, hy_g.reshape(1, -1), wa, wz, h, ln_g.reshape(1, D), ln_b.reshape(1, D))


def _arrange_w_in(w):
    q, k, v, ga, hy, gh = jnp.split(w, [1024, 1280, 1536, 2560, 5632], axis=1)
    pair = _head_pair_lane_perm()
    pq = np.concatenate([pair + LANES * i for i in range(Q_COLS // LANES)])
    pk = np.concatenate([pair + LANES * i for i in range(KV_COLS // LANES)])
    return jnp.concatenate([q[:, pq], ga, gh, hy, k[:, pk], v], axis=1).astype(BF16)


def _trunk(x, p):
    B, L, D = x.shape
    T = B * L
    pairs = B > 1
    dc = _dft_constants(L, pairs)
    cb, kb = (512, 16) if L <= 2048 else (256, 8)
    rope_tab = _rope_table(L)
    bias = _attn_bias()
    h, hb = _layernorm(x.reshape(T, D), p["emb_ln_g"], p["emb_ln_b"])
    for l in range(DEPTH):
        proj = _matmul(hb, _arrange_w_in(p["w_in"][l]), 1024, D_IN // 4, BF16)
        proj3 = proj.reshape(B, L, D_IN)
        a = _attention(proj3, rope_tab, bias, p["attn_sink"][l], p["attn_norm_g"][l])
        hyc = _short_conv(proj3, p["conv_w"][l], p["conv_b"][l])
        kf = _filter_spectrum(L, dc, p["flt_w1"][l], p["flt_b1"][l], p["flt_freq1"][l],
                              p["flt_w2"][l], p["flt_b2"][l], p["flt_freq2"][l],
                              p["flt_w3"][l], p["flt_decay"][l], cb, kb)
        z = _hyena(hyc, kf, dc, p["hyena_d"][l], cb, kb)
        h, hb = _out_proj(a.reshape(T, D_ATTN), z.reshape(T, D_HYENA), proj,
                          p["hyena_norm_g"][l], p["w_out"][l], h, p["ln_g"][l], p["ln_b"][l])
    return h.reshape(B, L, D)


def kernel(x_prompt, x_sample, emb_ln_g, emb_ln_b, w_in, attn_sink, conv_w, conv_b, flt_w1,
           flt_b1, flt_freq1, flt_w2, flt_b2, flt_freq2, flt_w3, flt_decay, hyena_d,
           attn_norm_g, hyena_norm_g, w_out, ln_g, ln_b):
    p = dict(emb_ln_g=emb_ln_g, emb_ln_b=emb_ln_b, w_in=w_in, attn_sink=attn_sink,
             conv_w=conv_w, conv_b=conv_b, flt_w1=flt_w1, flt_b1=flt_b1, flt_freq1=flt_freq1,
             flt_w2=flt_w2, flt_b2=flt_b2, flt_freq2=flt_freq2, flt_w3=flt_w3,
             flt_decay=flt_decay, hyena_d=hyena_d, attn_norm_g=attn_norm_g,
             hyena_norm_g=hyena_norm_g, w_out=w_out, ln_g=ln_g, ln_b=ln_b)
    return (_trunk(x_prompt, p), _trunk(x_sample, p))
```

```python
import functools
import math

import numpy as np
import jax
import jax.numpy as jnp
from jax.experimental import pallas as pl
from jax.experimental.pallas import tpu as pltpu

F32 = jnp.float32
BF16 = jnp.bfloat16

D_MODEL = 2048
DEPTH = 2
D_ATTN = 1024
D_HYENA = 1024
HEAD_DIM = 64
N_HEADS = 16
N_KV_HEADS = 4
GQA_GROUPS = 4
ROT_DIM = 16
ROPE_THETA = 500000.0
WINDOW = 128
BLOCK = 128
FILTER_BANDS = 16
FILTER_HIDDEN = 64
DN_ALPHA = (2.0 * DEPTH) ** 0.25
NORM_EPS = 1e-5
MASK_VALUE = -1e30
Q_COLS = N_HEADS * HEAD_DIM
KV_COLS = N_KV_HEADS * HEAD_DIM
D_IN = 2 * Q_COLS + 2 * KV_COLS + 4 * D_HYENA

COL_Q, COL_GA, COL_GH = 0, 1, 2
COL_K, COL_V = 12, 13
D_PLAIN = 3 * 1024 + 2 * KV_COLS

LANES = 128
SUB = 8
PACK = 16
GROUP = 2 * PACK
VMEM_LIMIT = 48 * 1024 * 1024


def _params(*sem):
    return pltpu.CompilerParams(dimension_semantics=sem, vmem_limit_bytes=VMEM_LIMIT)


def _ln_kernel(x_ref, g_ref, b_ref, o_ref, ob_ref):
    x = x_ref[...]
    mu = jnp.mean(x, axis=-1, keepdims=True)
    xc = x - mu
    var = jnp.mean(xc * xc, axis=-1, keepdims=True)
    y = xc * jax.lax.rsqrt(var + NORM_EPS) * g_ref[...] + b_ref[...]
    o_ref[...] = y
    ob_ref[...] = y.astype(BF16)


def _layernorm(x, g, b, tm=512):
    T, D = x.shape
    return pl.pallas_call(
        _ln_kernel,
        grid=(T // tm,),
        in_specs=[pl.BlockSpec((tm, D), lambda i: (i, 0)),
                  pl.BlockSpec((1, D), lambda i: (0, 0)),
                  pl.BlockSpec((1, D), lambda i: (0, 0))],
        out_specs=[pl.BlockSpec((tm, D), lambda i: (i, 0)),
                   pl.BlockSpec((tm, D), lambda i: (i, 0))],
        out_shape=[jax.ShapeDtypeStruct((T, D), F32), jax.ShapeDtypeStruct((T, D), BF16)],
        compiler_params=_params("parallel"),
        name="layernorm",
    )(x, g.reshape(1, D), b.reshape(1, D))


def _mm_kernel(x_ref, w_ref, o_ref):
    o_ref[...] = jnp.dot(x_ref[...], w_ref[...], preferred_element_type=F32).astype(o_ref.dtype)


def _matmul(x, w, tm, tn, out_dtype):
    M, K = x.shape
    N = w.shape[1]
    return pl.pallas_call(
        _mm_kernel,
        grid=(M // tm, N // tn),
        in_specs=[pl.BlockSpec((tm, K), lambda i, j: (i, 0)),
                  pl.BlockSpec((K, tn), lambda i, j: (0, j))],
        out_specs=pl.BlockSpec((tm, tn), lambda i, j: (i, j)),
        out_shape=jax.ShapeDtypeStruct((M, N), out_dtype),
        compiler_params=_params("parallel", "parallel"),
        name="in_proj",
    )(x, w)


_SLOT_OF_OCTET = (0, 4, 1, 2, 3, 5, 6, 7)


def _head_pair_lane_perm():
    perm = np.zeros(LANES, np.int32)
    for octet, slot in enumerate(_SLOT_OF_OCTET):
        for hd in range(2):
            for i in range(8):
                perm[16 * slot + 8 * hd + i] = hd * HEAD_DIM + 8 * octet + i
    return perm


def _rope(t, tab):
    return t * tab[0] + pltpu.roll(t, LANES // 2, 1) * tab[1]


def _attn_kernel(sink_ref, q_ref, ga_ref, kp_ref, kc_ref, kn_ref, vp_ref, vc_ref, vn_ref,
                 tp_ref, tc_ref, tn_ref, bias_ref, g_ref, o_ref, acc_ref):
    nkeys = 3 * BLOCK
    tabs = (tp_ref[...], tc_ref[...], tn_ref[...])
    krefs = (kp_ref, kc_ref, kn_ref)
    v3 = jnp.concatenate([vp_ref[...], vc_ref[...], vn_ref[...]], axis=0)
    lane = jax.lax.broadcasted_iota(jnp.int32, (1, LANES), 1)
    head_a = (lane % 16) < 8
    first_pair = jax.lax.broadcasted_iota(jnp.int32, (1, 2 * BLOCK), 1) < BLOCK
    scale = HEAD_DIM ** -0.5
    kcols = [jnp.concatenate(
        [_rope(krefs[w][:, c * LANES:(c + 1) * LANES].astype(F32), tabs[w]) for w in range(3)],
        axis=0) for c in range(KV_COLS // LANES)]

    for g in range(N_KV_HEADS):
        kcol = kcols[g // 2]
        if g % 2 == 0:
            k_a = jnp.where(head_a, kcol, 0.0)
            k_b = pltpu.roll(k_a, 8, 1)
        else:
            k_b = jnp.where(head_a, 0.0, kcol)
            k_a = pltpu.roll(k_b, LANES - 8, 1)
        kst = jnp.concatenate([k_a, k_b], axis=0).astype(BF16)
        q2 = jnp.concatenate(
            [(_rope(q_ref[:, (2 * g + pr) * LANES:(2 * g + pr + 1) * LANES].astype(F32), tabs[1])
              * scale).astype(BF16) for pr in range(2)], axis=0)
        st = jax.lax.dot_general(kst, q2, (((1,), (1,)), ((), ())), preferred_element_type=F32)
        st = st + bias_ref[...]
        vg = v3[:, g * HEAD_DIM:(g + 1) * HEAD_DIM]
        for par in range(2):
            s = st[par * nkeys:(par + 1) * nkeys]
            h0, h1 = 4 * g + par, 4 * g + 2 + par
            sk = jnp.where(first_pair, sink_ref[h0], sink_ref[h1])
            m = jnp.maximum(jnp.max(s, axis=0, keepdims=True), sk)
            p = jnp.exp(s - m)
            denom = jnp.sum(p, axis=0, keepdims=True) + jnp.exp(sk - m)
            pn = (p * (1.0 / denom)).astype(BF16)
            o = jax.lax.dot_general(pn, vg, (((0,), (0,)), ((), ())), preferred_element_type=F32)
            acc_ref[:, h0 * HEAD_DIM:(h0 + 1) * HEAD_DIM] = o[:BLOCK]
            acc_ref[:, h1 * HEAD_DIM:(h1 + 1) * HEAD_DIM] = o[BLOCK:]

    a = acc_ref[...]
    an = a * jax.lax.rsqrt(jnp.mean(a * a, axis=-1, keepdims=True) + NORM_EPS) * g_ref[...]
    ga = ga_ref[...].astype(F32)
    o_ref[...] = (an * (ga * jax.nn.sigmoid(ga))).astype(o_ref.dtype)


def _attention(proj3, rope_tab, bias, sink, attn_g):
    B, L, _ = proj3.shape
    nb = L // BLOCK
    assert nb >= 2
    prev = lambda n: jnp.maximum(n - 1, 0)
    nxt = lambda n: jnp.minimum(n + 1, nb - 1)
    which = (prev, lambda n: n, nxt)

    def kv_spec(col, w):
        return pl.BlockSpec((None, BLOCK, KV_COLS), lambda b, n: (b, which[w](n), col))

    def tab_spec(w):
        return pl.BlockSpec((2, BLOCK, LANES), lambda b, n: (0, which[w](n), 0))

    return pl.pallas_call(
        _attn_kernel,
        grid=(B, nb),
        in_specs=[pl.BlockSpec(memory_space=pltpu.SMEM),
                  pl.BlockSpec((None, BLOCK, Q_COLS), lambda b, n: (b, n, COL_Q)),
                  pl.BlockSpec((None, BLOCK, D_ATTN), lambda b, n: (b, n, COL_GA)),
                  kv_spec(COL_K, 0), kv_spec(COL_K, 1), kv_spec(COL_K, 2),
                  kv_spec(COL_V, 0), kv_spec(COL_V, 1), kv_spec(COL_V, 2),
                  tab_spec(0), tab_spec(1), tab_spec(2),
                  pl.BlockSpec((None, 6 * BLOCK, 2 * BLOCK),
                               lambda b, n: (jnp.where(n == 0, 0, jnp.where(n == nb - 1, 2, 1)), 0, 0)),
                  pl.BlockSpec((1, D_ATTN), lambda b, n: (0, 0))],
        out_specs=pl.BlockSpec((None, BLOCK, D_ATTN), lambda b, n: (b, n, 0)),
        out_shape=jax.ShapeDtypeStruct((B, L, D_ATTN), BF16),
        scratch_shapes=[pltpu.VMEM((BLOCK, D_ATTN), F32)],
        compiler_params=_params("parallel", "parallel"),
        name="window_attention",
    )(sink, proj3, proj3, proj3, proj3, proj3, proj3, proj3, proj3,
      rope_tab, rope_tab, rope_tab, bias, attn_g.reshape(1, D_ATTN))


def _rope_table(L):
    inv = ROPE_THETA ** (-jnp.arange(0, ROT_DIM, 2, dtype=F32) / ROT_DIM)
    ang = jnp.arange(L, dtype=F32)[:, None] * inv[None, :]
    cos, sin = jnp.cos(ang), jnp.sin(ang)
    ones, zeros = jnp.ones((L, 8), F32), jnp.zeros((L, 8), F32)
    c_slots = [cos if s in (0, 4) else ones for s in range(8)]
    s_slots = [-sin if s == 0 else (sin if s == 4 else zeros) for s in range(8)]
    c = jnp.concatenate([x for s in range(8) for x in (c_slots[s], c_slots[s])], axis=1)
    s = jnp.concatenate([x for s in range(8) for x in (s_slots[s], s_slots[s])], axis=1)
    return jnp.stack([c, s])


def _attn_bias():
    c = np.arange(3 * BLOCK)[:, None]
    r = np.arange(BLOCK)[None, :]
    band = (c >= r) & (c <= r + 2 * WINDOW)
    variants = [band & (c >= BLOCK), band, band & (c < 2 * BLOCK)]
    out = np.stack([np.tile(np.where(v, 0.0, MASK_VALUE), (2, 2)) for v in variants])
    return jnp.asarray(out.astype(np.float32))


HALO = PACK


def _proj_conv_kernel(xp_ref, x_ref, xn_ref, w_ref, cw_ref, cb_ref, o_ref, res_ref, *, L, chunk):
    tm = x_ref.shape[0]
    i = pl.program_id(0)
    xx = jnp.concatenate([xp_ref[...], x_ref[...], xn_ref[...]], axis=0)
    res_ref[...] = jnp.dot(xx, w_ref[...], preferred_element_type=F32)
    zeros = jnp.zeros((SUB, res_ref.shape[1]), F32)

    @pl.when((i * tm) % L == 0)
    def _():
        res_ref[HALO - SUB:HALO, :] = zeros

    @pl.when(((i + 1) * tm) % L == 0)
    def _():
        res_ref[HALO + tm:HALO + tm + SUB, :] = zeros

    w0, w1, w2, b = cw_ref[0:1, :], cw_ref[1:2, :], cw_ref[2:3, :], cb_ref[...]
    for r0 in range(0, tm, chunk):
        xm = res_ref[HALO - 1 + r0:HALO - 1 + r0 + chunk, :]
        x0 = res_ref[HALO + r0:HALO + r0 + chunk, :]
        xq = res_ref[HALO + 1 + r0:HALO + 1 + r0 + chunk, :]
        o_ref[r0:r0 + chunk, :] = (xm * w0 + x0 * w1 + xq * w2 + b).astype(o_ref.dtype)


def _proj_conv(x, w, conv_w, conv_b, L, tm=1024, tn=1024):
    T, K = x.shape
    N = w.shape[1]
    assert L % tm == 0 and tm % HALO == 0
    nh = T // HALO
    per = tm // HALO
    return pl.pallas_call(
        functools.partial(_proj_conv_kernel, L=L, chunk=256),
        grid=(T // tm, N // tn),
        in_specs=[pl.BlockSpec((HALO, K), lambda i, j: (jnp.maximum(i * per - 1, 0), 0)),
                  pl.BlockSpec((tm, K), lambda i, j: (i, 0)),
                  pl.BlockSpec((HALO, K), lambda i, j: (jnp.minimum((i + 1) * per, nh - 1), 0)),
                  pl.BlockSpec((K, tn), lambda i, j: (0, j)),
                  pl.BlockSpec((3, tn), lambda i, j: (0, j)),
                  pl.BlockSpec((1, tn), lambda i, j: (0, j))],
        out_specs=pl.BlockSpec((tm, tn), lambda i, j: (i, j)),
        out_shape=jax.ShapeDtypeStruct((T, N), BF16),
        scratch_shapes=[pltpu.VMEM((tm + 2 * HALO, tn), F32)],
        compiler_params=_params("parallel", "parallel"),
        name="in_proj_conv",
    )(x, x, x, w, conv_w, conv_b.reshape(1, -1))


def _split(a):
    hi = a.astype(BF16)
    return hi, (a - hi.astype(F32)).astype(BF16)


def _dot3(a, b):
    ah, al = _split(a)
    bh, bl = _split(b)
    d = lambda x, y: jnp.dot(x, y, preferred_element_type=F32)
    return d(ah, bh) + (d(ah, bl) + d(al, bh))


def _dot3_tn(at, b):
    ah, al = _split(at)
    bh, bl = _split(b)
    d = lambda x, y: jax.lax.dot_general(x, y, (((0,), (0,)), ((), ())), preferred_element_type=F32)
    return d(ah, bh) + (d(ah, bl) + d(al, bh))


def _fmlp_kernel(bands_ref, w1t_ref, w1c_ref, w1s_ref, b1_ref, f1_ref, w2_ref, b2_ref, f2_ref,
                 w3f_ref, w3b_ref, df_ref, db_ref, of_ref, ob_ref, nrm_ref, *, L, tr):
    i = pl.program_id(0)
    n = tr + LANES
    lag = (i * tr + jax.lax.broadcasted_iota(jnp.int32, (1, n), 1)).astype(F32)
    t_row = lag / (L - 1)
    ang = bands_ref[...] * (2.0 * math.pi * lag / L)
    pre = (w1t_ref[...] * t_row + _dot3(w1c_ref[...], jnp.cos(ang))
           + _dot3(w1s_ref[...], -jnp.sin(ang)) + b1_ref[...])
    h = jnp.sin(f1_ref[...] * pre)
    h = jnp.sin(f2_ref[...] * (_dot3(w2_ref[...], h) + b2_ref[...]))

    h_a = h[:, :tr]
    h_b = pltpu.roll(h, n - 1, 1)[:, :tr]
    pos = (i * tr + jax.lax.broadcasted_iota(jnp.int32, (tr, 1), 0)).astype(F32)
    t_a = pos / (L - 1)
    t_b = (pos + 1.0) / (L - 1)
    dec_b = jnp.abs(db_ref[...])
    out_f = _dot3_tn(h_a, w3f_ref[...]) * jnp.exp(-t_a * jnp.abs(df_ref[...]))
    out_b = _dot3_tn(h_b, w3b_ref[...]) * jnp.exp(-t_b * dec_b)
    out_b = jnp.where(pos + 1.0 <= L - 1, out_b, 0.0)
    of_ref[...] = out_f
    ob_ref[...] = out_b
    part = (jnp.sum(jnp.abs(out_f), axis=0, keepdims=True)
            + jnp.sum(jnp.abs(out_b), axis=0, keepdims=True))

    @pl.when(i == 0)
    def _():
        b0 = _dot3_tn(h[:, :LANES], w3b_ref[...])
        nrm_ref[...] = part + jnp.abs(b0[0:1])

    @pl.when(i > 0)
    def _():
        nrm_ref[...] += part


def _filter_mlp(L, w1, b1, f1, w2, b2, f2, w3, decay, tr=256):
    nf = 2 * D_HYENA
    w3r = w3.reshape(FILTER_HIDDEN, 2, 2, D_HYENA)
    dr = decay.reshape(2, 2, D_HYENA)
    w3f, w3b = w3r[:, :, 0].reshape(FILTER_HIDDEN, nf), w3r[:, :, 1].reshape(FILTER_HIDDEN, nf)
    df, db = dr[:, 0].reshape(1, nf), dr[:, 1].reshape(1, nf)
    bands = jnp.linspace(1e-4, FILTER_BANDS - 1, FILTER_BANDS, dtype=F32).reshape(FILTER_BANDS, 1)
    H = FILTER_HIDDEN
    col = lambda v: v.reshape(H, 1)
    full = lambda shape: pl.BlockSpec(shape, lambda i: (0,) * len(shape))
    return pl.pallas_call(
        functools.partial(_fmlp_kernel, L=L, tr=tr),
        grid=(L // tr,),
        in_specs=[full((FILTER_BANDS, 1)), full((H, 1)), full((H, FILTER_BANDS)),
                  full((H, FILTER_BANDS)), full((H, 1)), full((H, 1)), full((H, H)),
                  full((H, 1)), full((H, 1)), full((H, nf)), full((H, nf)),
                  full((1, nf)), full((1, nf))],
        out_specs=[pl.BlockSpec((tr, nf), lambda i: (i, 0)),
                   pl.BlockSpec((tr, nf), lambda i: (i, 0)),
                   pl.BlockSpec((1, nf), lambda i: (0, 0))],
        out_shape=[jax.ShapeDtypeStruct((L, nf), F32), jax.ShapeDtypeStruct((L, nf), F32),
                   jax.ShapeDtypeStruct((1, nf), F32)],
        compiler_params=_params("arbitrary"),
        name="filter_mlp",
    )(bands, w1[0:1].T, w1[1:1 + FILTER_BANDS].T, w1[1 + FILTER_BANDS:].T, col(b1), col(f1),
      w2.T, col(b2), col(f2), w3f, w3b, df, db)


def _cs(num, den):
    ang = 2.0 * np.pi * (np.asarray(num, np.int64) % den).astype(np.float64) / den
    return np.cos(ang), np.sin(ang)


def _dft_constants(L, pairs):
    n = 2 * L
    n1 = 128 if L >= 8192 else 64
    n2 = n // n1
    n1h = n1 // 2
    q = n1h if pairs else n1 // 4
    ar = np.arange
    eye = np.eye(SUB)
    bf = lambda m: jnp.asarray(m.astype(np.float32)).astype(BF16)

    c, s = _cs(ar(n1)[:, None] * ar(q)[None, :], n1)
    f1 = np.stack([np.concatenate([c, s], axis=1), np.concatenate([-s, c], axis=1)], axis=1)
    k1 = np.kron(f1.reshape(2 * n1, 2 * q), eye)

    c, s = _cs(ar(n1)[:, None] * ar(n1h)[None, :], n1)
    k1f = np.kron(np.stack([c, -s], axis=1).reshape(2 * n1, n1h), eye)
    c, s = _cs(ar(n1)[:, None] * (n1 - 1 - ar(n1h))[None, :], n1)
    k1b = np.kron(np.stack([c, -s], axis=1).reshape(2 * n1, n1h), eye[::-1])

    c2, s2 = (jnp.asarray(x.astype(np.float32)) for x in _cs(ar(n2)[:, None] * ar(n2)[None, :], n2))
    ct, st = (jnp.asarray(x.astype(np.float32)) for x in _cs(ar(n1)[:, None] * ar(n2)[None, :], n))
    gr = c2[None] * ct[:, None, :] - s2[None] * st[:, None, :]
    gi = -(s2[None] * ct[:, None, :] + c2[None] * st[:, None, :])
    gs = jnp.concatenate([jnp.concatenate([gr, -gi], axis=2),
                          jnp.concatenate([gi, gr], axis=2)], axis=1).astype(BF16)
    gst = jnp.swapaxes(gs, 1, 2)

    c, s = _cs(ar(n1h)[:, None] * ar(n1)[None, :], n1)
    if pairs:
        top = np.stack([c, -s], axis=2).reshape(n1h, 2 * n1)
        bot = np.stack([s, c], axis=2).reshape(n1h, 2 * n1)
        f3 = np.concatenate([top, bot], axis=0)
    else:
        c2h, s2h = _cs((ar(n1h)[:, None] - q) * ar(n1)[None, :], n1)
        f3 = np.stack([c + s2h, -s + c2h], axis=2).reshape(n1h, 2 * n1)
    k3 = np.kron(f3 / n, eye)
    return dict(n1=n1, n2=n2, q=q, k1=bf(k1), k1f=bf(k1f), k1b=bf(k1b), gs=gs, gst=gst, k3=bf(k3))


def _halves(x):
    cb = x.shape[-1]
    return [x[..., h * SUB:(h + 1) * SUB, :].reshape(-1, cb).astype(BF16) for h in range(2)]


def _join(lo, hi, lead):
    cb = lo.shape[-1]
    return jnp.concatenate([lo.reshape(*lead, SUB, cb), hi.reshape(*lead, SUB, cb)], axis=len(lead))


INNER_LANES = 256


def _inner_forward(scr_ref, k1_idx, c0, gs):
    n2 = scr_ref.shape[2]
    a = scr_ref[k1_idx, :, :, c0:c0 + INNER_LANES].reshape(2 * n2, INNER_LANES)
    return jnp.dot(gs, a, preferred_element_type=F32)


def _conv_kernel(u_ref, g_ref, d_ref, k1_ref, k3_ref, gs_ref, gst_ref, kf_ref, o_ref, scr_ref,
                 *, s1, s2):
    n1, _, n2, cb = scr_ref.shape
    kb = gs_ref.shape[0]
    q = u_ref.shape[1]
    s = pl.program_id(2)

    @pl.when(s < s1)
    def _():
        for p0 in range(0, GROUP, PACK):
            x = u_ref[:, :, p0:p0 + PACK, :].astype(F32)
            lo, hi = [jnp.dot(k1_ref[...], xh, preferred_element_type=F32) for xh in _halves(x)]
            row0 = pl.multiple_of(s * GROUP + p0, PACK)
            scr_ref[:, :, pl.ds(row0, PACK), :] = _join(lo, hi, (n1, 2)).astype(BF16)

    @pl.when((s >= s1) & (s < s1 + s2))
    def _():
        for kk in range(kb):
            k1_idx = (s - s1) * kb + kk
            for c0 in range(0, cb, INNER_LANES):
                u = _inner_forward(scr_ref, k1_idx, c0, gs_ref[kk])
                ur, ui = u[:n2], u[n2:]
                kr = kf_ref[kk, 0, :, c0:c0 + INNER_LANES].astype(F32)
                ki = kf_ref[kk, 1, :, c0:c0 + INNER_LANES].astype(F32)
                v = jnp.concatenate([ur * kr - ui * ki, ur * ki + ui * kr], axis=0).astype(BF16)
                b = jnp.dot(gst_ref[kk], v, preferred_element_type=F32)
                scr_ref[k1_idx, :, :, c0:c0 + INNER_LANES] = (
                    b.reshape(2, n2, INNER_LANES).astype(BF16))

    @pl.when(s >= s1 + s2)
    def _():
        for p0 in range(0, GROUP, PACK):
            row0 = pl.multiple_of((s - s1 - s2) * GROUP + p0, PACK)
            b = scr_ref[:, :, pl.ds(row0, PACK), :].astype(F32)
            lo, hi = [jnp.dot(k3_ref[...], x, preferred_element_type=F32) for x in _halves(b)]
            y = _join(lo, hi, (2, q))
            rows = slice(p0, p0 + PACK)
            z = g_ref[:, :, rows, :].astype(F32) * (y + u_ref[:, :, rows, :].astype(F32) * d_ref[...])
            o_ref[:, :, rows, :] = z.astype(o_ref.dtype)


def _long_conv(u6, u_off, g6, g_off, d, kf, order, dc, cb, kb):
    bp, _, q, n2p, _, _ = u6.shape
    C = D_HYENA
    n1, n2 = dc["n1"], dc["n2"]
    s1, s2 = n2p, n1 // kb
    koff = order * (C // cb)
    grp1 = lambda s: jnp.where(s < s1, s, jnp.where(s < s1 + s2, s1 - 1, s - s1 - s2))
    grp3 = lambda s: jnp.maximum(s - s1 - s2, 0)
    kblk = lambda s: jnp.clip(s - s1, 0, s2 - 1)
    tspec = lambda off, grp: pl.BlockSpec((None, 2, q, None, GROUP, cb),
                                          lambda b, c, s: (b, 0, 0, grp(s), 0, off + c))
    const = lambda a: pl.BlockSpec(a.shape, lambda b, c, s: (0,) * a.ndim,
                                   pipeline_mode=pl.Buffered(1))
    return pl.pallas_call(
        functools.partial(_conv_kernel, s1=s1, s2=s2),
        grid=(bp, C // cb, s1 + s2 + s1),
        in_specs=[tspec(u_off, grp1), tspec(g_off, grp3),
                  pl.BlockSpec((1, cb), lambda b, c, s: (0, c)),
                  const(dc["k1"]), const(dc["k3"]),
                  pl.BlockSpec((kb, 2 * n2, 2 * n2), lambda b, c, s: (kblk(s), 0, 0)),
                  pl.BlockSpec((kb, 2 * n2, 2 * n2), lambda b, c, s: (kblk(s), 0, 0)),
                  pl.BlockSpec((kb, 2, n2, cb), lambda b, c, s: (kblk(s), 0, 0, koff + c))],
        out_specs=tspec(0, grp3),
        out_shape=jax.ShapeDtypeStruct((bp, 2, q, n2p, GROUP, C), BF16),
        scratch_shapes=[pltpu.VMEM((n1, 2, n2, cb), BF16)],
        compiler_params=_params("parallel", "parallel", "arbitrary"),
        name="long_conv",
    )(u6, g6, d.reshape(1, C), dc["k1"], dc["k3"], dc["gs"], dc["gst"], kf)


def _hyena(hyc, kf, dc, hy_d, cb, kb):
    B, L, _ = hyc.shape
    C = D_HYENA
    n2, q = dc["n2"], dc["q"]
    bp = B * L // (2 * q * n2)
    hy6 = hyc.reshape(bp, 2, q, n2 // GROUP, GROUP, 3 * C)
    z6 = _long_conv(hy6, 0, hy6, C // cb, hy_d[0], kf, 0, dc, cb, kb)
    z6 = _long_conv(z6, 0, hy6, 2 * (C // cb), hy_d[1], kf, 1, dc, cb, kb)
    return z6.reshape(B, L, C)


def _filt_kernel(xf_ref, xb_ref, nrm_ref, k1f_ref, k1b_ref, gs_ref, o_ref, scr_ref, *, s1):
    n1, _, n2, cb = scr_ref.shape
    kb = gs_ref.shape[0]
    s = pl.program_id(1)

    @pl.when(s < s1)
    def _():
        d = lambda k, x: jnp.dot(k[...], x, preferred_element_type=F32)
        for p0 in range(0, GROUP, PACK):
            f_lo, f_hi = _halves(xf_ref[:, p0:p0 + PACK, :])
            b_lo, b_hi = _halves(xb_ref[:, GROUP - PACK - p0:GROUP - p0, :])
            lo = d(k1f_ref, f_lo) + d(k1b_ref, b_hi)
            hi = d(k1f_ref, f_hi) + d(k1b_ref, b_lo)
            row0 = pl.multiple_of(s * GROUP + p0, PACK)
            scr_ref[:, :, pl.ds(row0, PACK), :] = _join(lo, hi, (n1, 2)).astype(BF16)

    @pl.when(s >= s1)
    def _():
        for kk in range(kb):
            for c0 in range(0, cb, INNER_LANES):
                inv = 1.0 / nrm_ref[:, c0:c0 + INNER_LANES]
                u = _inner_forward(scr_ref, (s - s1) * kb + kk, c0, gs_ref[kk]) * inv
                o_ref[kk, :, :, c0:c0 + INNER_LANES] = (
                    u.reshape(2, n2, INNER_LANES).astype(o_ref.dtype))


def _filter_spectrum(L, dc, fw1, fb1, ff1, fw2, fb2, ff2, fw3, fdecay, cb, kb):
    hf, hb, nrm = _filter_mlp(L, fw1, fb1, ff1, fw2, fb2, ff2, fw3, fdecay)
    nf = hf.shape[1]
    n1, n2 = dc["n1"], dc["n2"]
    n1h, n2p = n1 // 2, n2 // GROUP
    s1, s2 = n2p, n1 // kb
    hf4 = hf.reshape(n1h, n2p, GROUP, nf)
    hb4 = hb.reshape(n1h, n2p, GROUP, nf)
    grp = lambda s: jnp.minimum(s, s1 - 1)
    kblk = lambda s: jnp.maximum(s - s1, 0)
    const = lambda a: pl.BlockSpec(a.shape, lambda c, s: (0,) * a.ndim,
                                   pipeline_mode=pl.Buffered(1))
    return pl.pallas_call(
        functools.partial(_filt_kernel, s1=s1),
        grid=(nf // cb, s1 + s2),
        in_specs=[pl.BlockSpec((n1h, None, GROUP, cb), lambda c, s: (0, grp(s), 0, c)),
                  pl.BlockSpec((n1h, None, GROUP, cb), lambda c, s: (0, s1 - 1 - grp(s), 0, c)),
                  pl.BlockSpec((1, cb), lambda c, s: (0, c)),
                  const(dc["k1f"]), const(dc["k1b"]),
                  pl.BlockSpec((kb, 2 * n2, 2 * n2), lambda c, s: (kblk(s), 0, 0))],
        out_specs=pl.BlockSpec((kb, 2, n2, cb), lambda c, s: (kblk(s), 0, 0, c)),
        out_shape=jax.ShapeDtypeStruct((n1, 2, n2, nf), BF16),
        scratch_shapes=[pltpu.VMEM((n1, 2, n2, cb), BF16)],
        compiler_params=_params("parallel", "arbitrary"),
        name="filter_spectrum",
    )(hf4, hb4, nrm, dc["k1f"], dc["k1b"], dc["gs"])


OUT_CHAINS = 2


def _out_kernel(a_ref, z_ref, gh_ref, hg_ref, wa_ref, wz_ref, h_ref, g_ref, b_ref, o_ref, ob_ref):
    chain = a_ref.shape[0] // OUT_CHAINS
    for r0 in range(0, a_ref.shape[0], chain):
        rows = slice(r0, r0 + chain)
        z = z_ref[rows, :].astype(F32)
        gh = gh_ref[rows, :].astype(F32)
        zn = z * jax.lax.rsqrt(jnp.mean(z * z, axis=-1, keepdims=True) + NORM_EPS) * hg_ref[...]
        zn = (zn * (gh * jax.nn.sigmoid(gh))).astype(BF16)
        acc = (jnp.dot(a_ref[rows, :], wa_ref[...], preferred_element_type=F32)
               + jnp.dot(zn, wz_ref[...], preferred_element_type=F32))
        y = DN_ALPHA * h_ref[rows, :] + acc
        mu = jnp.mean(y, axis=-1, keepdims=True)
        yc = y - mu
        var = jnp.mean(yc * yc, axis=-1, keepdims=True)
        out = yc * jax.lax.rsqrt(var + NORM_EPS) * g_ref[...] + b_ref[...]
        o_ref[rows, :] = out
        ob_ref[rows, :] = out.astype(BF16)


def _out_proj(a, z, proj, hy_g, w_out, h, ln_g, ln_b, tm=512):
    T, D = h.shape
    wa = w_out[:D_ATTN].astype(BF16)
    wz = w_out[D_ATTN:].astype(BF16)
    row = lambda w: pl.BlockSpec((tm, w), lambda i: (i, 0))
    const = lambda r, w: pl.BlockSpec((r, w), lambda i: (0, 0), pipeline_mode=pl.Buffered(1))
    return pl.pallas_call(
        _out_kernel,
        grid=(T // tm,),
        in_specs=[row(D_ATTN), row(D_HYENA),
                  pl.BlockSpec((tm, D_HYENA), lambda i: (i, COL_GH)),
                  const(1, D_HYENA), const(D_ATTN, D), const(D_HYENA, D), row(D),
                  const(1, D), const(1, D)],
        out_specs=[row(D), row(D)],
        out_shape=[jax.ShapeDtypeStruct((T, D), F32), jax.ShapeDtypeStruct((T, D), BF16)],
        compiler_params=_params("parallel"),
        name="out_proj",
    )(a, z, proj, hy_g.reshape(1, -1), wa, wz, h, ln_g.reshape(1, D), ln_b.reshape(1, D))


def _arrange_w_in(w):
    q, k, v, ga, hy, gh = jnp.split(w, [1024, 1280, 1536, 2560, 5632], axis=1)
    pair = _head_pair_lane_perm()
    pq = np.concatenate([pair + LANES * i for i in range(Q_COLS // LANES)])
    pk = np.concatenate([pair + LANES * i for i in range(KV_COLS // LANES)])
    plain = jnp.concatenate([q[:, pq], ga, gh, k[:, pk], v], axis=1).astype(BF16)
    return plain, hy.astype(BF16)


def _trunk(x, p):
    B, L, D = x.shape
    T = B * L
    pairs = B > 1
    dc = _dft_constants(L, pairs)
    cb, kb, kb_filt = (512, 32, 32) if L <= 2048 else (256, 8, 16)
    rope_tab = _rope_table(L)
    bias = _attn_bias()
    h, hb = _layernorm(x.reshape(T, D), p["emb_ln_g"], p["emb_ln_b"])
    for l in range(DEPTH):
        w_plain, w_hy = _arrange_w_in(p["w_in"][l])
        proj = _matmul(hb, w_plain, 1024, D_PLAIN // 2, BF16)
        proj3 = proj.reshape(B, L, D_PLAIN)
        a = _attention(proj3, rope_tab, bias, p["attn_sink"][l], p["attn_norm_g"][l])
        hyc = _proj_conv(hb, w_hy, p["conv_w"][l], p["conv_b"][l], L).reshape(B, L, 3 * D_HYENA)
        kf = _filter_spectrum(L, dc, p["flt_w1"][l], p["flt_b1"][l], p["flt_freq1"][l],
                              p["flt_w2"][l], p["flt_b2"][l], p["flt_freq2"][l],
                              p["flt_w3"][l], p["flt_decay"][l], cb, kb_filt)
        z = _hyena(hyc, kf, dc, p["hyena_d"][l], cb, kb)
        h, hb = _out_proj(a.reshape(T, D_ATTN), z.reshape(T, D_HYENA), proj,
                          p["hyena_norm_g"][l], p["w_out"][l], h, p["ln_g"][l], p["ln_b"][l])
    return h.reshape(B, L, D)


def kernel(x_prompt, x_sample, emb_ln_g, emb_ln_b, w_in, attn_sink, conv_w, conv_b, flt_w1,
           flt_b1, flt_freq1, flt_w2, flt_b2, flt_freq2, flt_w3, flt_decay, hyena_d,
           attn_norm_g, hyena_norm_g, w_out, ln_g, ln_b):
    p = dict(emb_ln_g=emb_ln_g, emb_ln_b=emb_ln_b, w_in=w_in, attn_sink=attn_sink,
             conv_w=conv_w, conv_b=conv_b, flt_w1=flt_w1, flt_b1=flt_b1, flt_freq1=flt_freq1,
             flt_w2=flt_w2, flt_b2=flt_b2, flt_freq2=flt_freq2, flt_w3=flt_w3,
             flt_decay=flt_decay, hyena_d=hyena_d, attn_norm_g=attn_norm_g,
             hyena_norm_g=hyena_norm_g, w_out=w_out, ln_g=ln_g, ln_b=ln_b)
    return (_trunk(x_prompt, p), _trunk(x_sample, p))
```

```python
import functools
import math

import numpy as np
import jax
import jax.numpy as jnp
from jax.experimental import pallas as pl
from jax.experimental.pallas import tpu as pltpu

F32 = jnp.float32
BF16 = jnp.bfloat16

D_MODEL = 2048
DEPTH = 2
D_ATTN = 1024
D_HYENA = 1024
HEAD_DIM = 64
N_HEADS = 16
N_KV_HEADS = 4
GQA_GROUPS = 4
ROT_DIM = 16
ROPE_THETA = 500000.0
WINDOW = 128
BLOCK = 128
FILTER_BANDS = 16
FILTER_HIDDEN = 64
DN_ALPHA = (2.0 * DEPTH) ** 0.25
NORM_EPS = 1e-5
MASK_VALUE = -1e30
Q_COLS = N_HEADS * HEAD_DIM
KV_COLS = N_KV_HEADS * HEAD_DIM
D_IN = 2 * Q_COLS + 2 * KV_COLS + 4 * D_HYENA

COL_Q, COL_GA, COL_GH = 0, 1, 2
COL_K, COL_V = 12, 13
D_PLAIN = 3 * 1024 + 2 * KV_COLS

LANES = 128
SUB = 8
PACK = 16
GROUP = 2 * PACK
VMEM_LIMIT = 48 * 1024 * 1024


def _params(*sem):
    return pltpu.CompilerParams(dimension_semantics=sem, vmem_limit_bytes=VMEM_LIMIT)


def _ln_kernel(x_ref, g_ref, b_ref, o_ref, ob_ref):
    x = x_ref[...]
    mu = jnp.mean(x, axis=-1, keepdims=True)
    xc = x - mu
    var = jnp.mean(xc * xc, axis=-1, keepdims=True)
    y = xc * jax.lax.rsqrt(var + NORM_EPS) * g_ref[...] + b_ref[...]
    o_ref[...] = y
    ob_ref[...] = y.astype(BF16)


def _layernorm(x, g, b, tm=512):
    T, D = x.shape
    return pl.pallas_call(
        _ln_kernel,
        grid=(T // tm,),
        in_specs=[pl.BlockSpec((tm, D), lambda i: (i, 0)),
                  pl.BlockSpec((1, D), lambda i: (0, 0)),
                  pl.BlockSpec((1, D), lambda i: (0, 0))],
        out_specs=[pl.BlockSpec((tm, D), lambda i: (i, 0)),
                   pl.BlockSpec((tm, D), lambda i: (i, 0))],
        out_shape=[jax.ShapeDtypeStruct((T, D), F32), jax.ShapeDtypeStruct((T, D), BF16)],
        compiler_params=_params("parallel"),
        name="layernorm",
    )(x, g.reshape(1, D), b.reshape(1, D))


def _mm_kernel(x_ref, w_ref, o_ref):
    o_ref[...] = jnp.dot(x_ref[...], w_ref[...], preferred_element_type=F32).astype(o_ref.dtype)


def _matmul(x, w, tm, tn, out_dtype):
    M, K = x.shape
    N = w.shape[1]
    return pl.pallas_call(
        _mm_kernel,
        grid=(M // tm, N // tn),
        in_specs=[pl.BlockSpec((tm, K), lambda i, j: (i, 0)),
                  pl.BlockSpec((K, tn), lambda i, j: (0, j))],
        out_specs=pl.BlockSpec((tm, tn), lambda i, j: (i, j)),
        out_shape=jax.ShapeDtypeStruct((M, N), out_dtype),
        compiler_params=_params("parallel", "parallel"),
        name="in_proj",
    )(x, w)


_SLOT_OF_OCTET = (0, 4, 1, 2, 3, 5, 6, 7)


def _head_pair_lane_perm():
    perm = np.zeros(LANES, np.int32)
    for octet, slot in enumerate(_SLOT_OF_OCTET):
        for hd in range(2):
            for i in range(8):
                perm[16 * slot + 8 * hd + i] = hd * HEAD_DIM + 8 * octet + i
    return perm


def _rope(t, tab):
    return t * tab[0] + pltpu.roll(t, LANES // 2, 1) * tab[1]


def _attn_kernel(sink_ref, q_ref, ga_ref, kp_ref, kc_ref, kn_ref, vp_ref, vc_ref, vn_ref,
                 tp_ref, tc_ref, tn_ref, bias_ref, g_ref, o_ref, acc_ref):
    nkeys = 3 * BLOCK
    tabs = (tp_ref[...], tc_ref[...], tn_ref[...])
    krefs = (kp_ref, kc_ref, kn_ref)
    v3 = jnp.concatenate([vp_ref[...], vc_ref[...], vn_ref[...]], axis=0)
    lane = jax.lax.broadcasted_iota(jnp.int32, (1, LANES), 1)
    head_a = (lane % 16) < 8
    first_pair = jax.lax.broadcasted_iota(jnp.int32, (1, 2 * BLOCK), 1) < BLOCK
    scale = HEAD_DIM ** -0.5
    kcols = [jnp.concatenate(
        [_rope(krefs[w][:, c * LANES:(c + 1) * LANES].astype(F32), tabs[w]) for w in range(3)],
        axis=0) for c in range(KV_COLS // LANES)]

    for g in range(N_KV_HEADS):
        kcol = kcols[g // 2]
        if g % 2 == 0:
            k_a = jnp.where(head_a, kcol, 0.0)
            k_b = pltpu.roll(k_a, 8, 1)
        else:
            k_b = jnp.where(head_a, 0.0, kcol)
            k_a = pltpu.roll(k_b, LANES - 8, 1)
        kst = jnp.concatenate([k_a, k_b], axis=0).astype(BF16)
        q2 = jnp.concatenate(
            [(_rope(q_ref[:, (2 * g + pr) * LANES:(2 * g + pr + 1) * LANES].astype(F32), tabs[1])
              * scale).astype(BF16) for pr in range(2)], axis=0)
        st = jax.lax.dot_general(kst, q2, (((1,), (1,)), ((), ())), preferred_element_type=F32)
        st = st + bias_ref[...]
        vg = v3[:, g * HEAD_DIM:(g + 1) * HEAD_DIM]
        for par in range(2):
            s = st[par * nkeys:(par + 1) * nkeys]
            h0, h1 = 4 * g + par, 4 * g + 2 + par
            sk = jnp.where(first_pair, sink_ref[h0], sink_ref[h1])
            m = jnp.maximum(jnp.max(s, axis=0, keepdims=True), sk)
            p = jnp.exp(s - m)
            denom = jnp.sum(p, axis=0, keepdims=True) + jnp.exp(sk - m)
            pn = (p * (1.0 / denom)).astype(BF16)
            o = jax.lax.dot_general(pn, vg, (((0,), (0,)), ((), ())), preferred_element_type=F32)
            acc_ref[:, h0 * HEAD_DIM:(h0 + 1) * HEAD_DIM] = o[:BLOCK]
            acc_ref[:, h1 * HEAD_DIM:(h1 + 1) * HEAD_DIM] = o[BLOCK:]

    a = acc_ref[...]
    an = a * jax.lax.rsqrt(jnp.mean(a * a, axis=-1, keepdims=True) + NORM_EPS) * g_ref[...]
    ga = ga_ref[...].astype(F32)
    o_ref[...] = (an * (ga * jax.nn.sigmoid(ga))).astype(o_ref.dtype)


def _attention(proj3, rope_tab, bias, sink, attn_g):
    B, L, _ = proj3.shape
    nb = L // BLOCK
    assert nb >= 2
    prev = lambda n: jnp.maximum(n - 1, 0)
    nxt = lambda n: jnp.minimum(n + 1, nb - 1)
    which = (prev, lambda n: n, nxt)

    def kv_spec(col, w):
        return pl.BlockSpec((None, BLOCK, KV_COLS), lambda b, n: (b, which[w](n), col))

    def tab_spec(w):
        return pl.BlockSpec((2, BLOCK, LANES), lambda b, n: (0, which[w](n), 0))

    return pl.pallas_call(
        _attn_kernel,
        grid=(B, nb),
        in_specs=[pl.BlockSpec(memory_space=pltpu.SMEM),
                  pl.BlockSpec((None, BLOCK, Q_COLS), lambda b, n: (b, n, COL_Q)),
                  pl.BlockSpec((None, BLOCK, D_ATTN), lambda b, n: (b, n, COL_GA)),
                  kv_spec(COL_K, 0), kv_spec(COL_K, 1), kv_spec(COL_K, 2),
                  kv_spec(COL_V, 0), kv_spec(COL_V, 1), kv_spec(COL_V, 2),
                  tab_spec(0), tab_spec(1), tab_spec(2),
                  pl.BlockSpec((None, 6 * BLOCK, 2 * BLOCK),
                               lambda b, n: (jnp.where(n == 0, 0, jnp.where(n == nb - 1, 2, 1)), 0, 0)),
                  pl.BlockSpec((1, D_ATTN), lambda b, n: (0, 0))],
        out_specs=pl.BlockSpec((None, BLOCK, D_ATTN), lambda b, n: (b, n, 0)),
        out_shape=jax.ShapeDtypeStruct((B, L, D_ATTN), BF16),
        scratch_shapes=[pltpu.VMEM((BLOCK, D_ATTN), F32)],
        compiler_params=_params("parallel", "parallel"),
        name="window_attention",
    )(sink, proj3, proj3, proj3, proj3, proj3, proj3, proj3, proj3,
      rope_tab, rope_tab, rope_tab, bias, attn_g.reshape(1, D_ATTN))


def _rope_table(L):
    inv = ROPE_THETA ** (-jnp.arange(0, ROT_DIM, 2, dtype=F32) / ROT_DIM)
    lane = np.arange(LANES)
    slot = lane // 16
    rotary = (slot == 0) | (slot == 4)
    inv_lane = jnp.where(jnp.asarray(rotary), inv[lane % 8], 0.0)
    sign = jnp.asarray(np.where(slot == 0, -1.0, np.where(slot == 4, 1.0, 0.0)).astype(np.float32))
    ang = jnp.arange(L, dtype=F32)[:, None] * inv_lane[None, :]
    return jnp.stack([jnp.cos(ang), jnp.sin(ang) * sign[None, :]])


def _attn_bias():
    c = np.arange(3 * BLOCK)[:, None]
    r = np.arange(BLOCK)[None, :]
    band = (c >= r) & (c <= r + 2 * WINDOW)
    variants = [band & (c >= BLOCK), band, band & (c < 2 * BLOCK)]
    out = np.stack([np.tile(np.where(v, 0.0, MASK_VALUE), (2, 2)) for v in variants])
    return jnp.asarray(out.astype(np.float32))


HALO = PACK


def _proj_conv_kernel(xp_ref, x_ref, xn_ref, w_ref, cw_ref, cb_ref, o_ref, res_ref, *, L, chunk):
    tm = x_ref.shape[0]
    i = pl.program_id(0)
    xx = jnp.concatenate([xp_ref[...], x_ref[...], xn_ref[...]], axis=0)
    res_ref[...] = jnp.dot(xx, w_ref[...], preferred_element_type=F32)
    before = res_ref[HALO - SUB:HALO, :]
    res_ref[HALO - SUB:HALO, :] = jnp.where((i * tm) % L == 0, 0.0, before)
    after = res_ref[HALO + tm:HALO + tm + SUB, :]
    res_ref[HALO + tm:HALO + tm + SUB, :] = jnp.where(((i + 1) * tm) % L == 0, 0.0, after)

    w0, w1, w2, b = cw_ref[0:1, :], cw_ref[1:2, :], cw_ref[2:3, :], cb_ref[...]
    for r0 in range(0, tm, chunk):
        xm = res_ref[HALO - 1 + r0:HALO - 1 + r0 + chunk, :]
        x0 = res_ref[HALO + r0:HALO + r0 + chunk, :]
        xq = res_ref[HALO + 1 + r0:HALO + 1 + r0 + chunk, :]
        o_ref[r0:r0 + chunk, :] = (xm * w0 + x0 * w1 + xq * w2 + b).astype(o_ref.dtype)


def _proj_conv(x, w, conv_w, conv_b, L, tm=1024, tn=1024):
    T, K = x.shape
    N = w.shape[1]
    assert L % tm == 0 and tm % HALO == 0
    nh = T // HALO
    per = tm // HALO
    return pl.pallas_call(
        functools.partial(_proj_conv_kernel, L=L, chunk=256),
        grid=(T // tm, N // tn),
        in_specs=[pl.BlockSpec((HALO, K), lambda i, j: (jnp.maximum(i * per - 1, 0), 0)),
                  pl.BlockSpec((tm, K), lambda i, j: (i, 0)),
                  pl.BlockSpec((HALO, K), lambda i, j: (jnp.minimum((i + 1) * per, nh - 1), 0)),
                  pl.BlockSpec((K, tn), lambda i, j: (0, j)),
                  pl.BlockSpec((3, tn), lambda i, j: (0, j)),
                  pl.BlockSpec((1, tn), lambda i, j: (0, j))],
        out_specs=pl.BlockSpec((tm, tn), lambda i, j: (i, j)),
        out_shape=jax.ShapeDtypeStruct((T, N), BF16),
        scratch_shapes=[pltpu.VMEM((tm + 2 * HALO, tn), F32)],
        compiler_params=_params("parallel", "parallel"),
        name="in_proj_conv",
    )(x, x, x, w, conv_w, conv_b.reshape(1, -1))


def _split(a):
    hi = a.astype(BF16)
    return hi, (a - hi.astype(F32)).astype(BF16)


def _dot3(a, b):
    ah, al = _split(a)
    bh, bl = _split(b)
    d = lambda x, y: jnp.dot(x, y, preferred_element_type=F32)
    return d(ah, bh) + (d(ah, bl) + d(al, bh))


def _dot3_tn(at, b):
    ah, al = _split(at)
    bh, bl = _split(b)
    a3 = jnp.concatenate([ah, al, ah], axis=0)
    b3 = jnp.concatenate([bh, bh, bl], axis=0)
    return jax.lax.dot_general(a3, b3, (((0,), (0,)), ((), ())), preferred_element_type=F32)


def _fmlp_kernel(bands_ref, w1t_ref, w1c_ref, w1s_ref, b1_ref, f1_ref, w2_ref, b2_ref, f2_ref,
                 w3f_ref, w3b_ref, df_ref, db_ref, of_ref, ob_ref, nrm_ref, *, L, tr):
    i = pl.program_id(0)
    n = tr + LANES
    lag = (i * tr + jax.lax.broadcasted_iota(jnp.int32, (1, n), 1)).astype(F32)
    t_row = lag / (L - 1)
    ang = bands_ref[...] * (2.0 * math.pi * lag / L)
    pre = (w1t_ref[...] * t_row + _dot3(w1c_ref[...], jnp.cos(ang))
           + _dot3(w1s_ref[...], -jnp.sin(ang)) + b1_ref[...])
    h = jnp.sin(f1_ref[...] * pre)
    h = jnp.sin(f2_ref[...] * (_dot3(w2_ref[...], h) + b2_ref[...]))

    h_a = h[:, :tr]
    h_b = pltpu.roll(h, n - 1, 1)[:, :tr]
    pos = (i * tr + jax.lax.broadcasted_iota(jnp.int32, (tr, 1), 0)).astype(F32)
    t_a = pos / (L - 1)
    t_b = (pos + 1.0) / (L - 1)
    dec_b = jnp.abs(db_ref[...])
    out_f = _dot3_tn(h_a, w3f_ref[...]) * jnp.exp(-t_a * jnp.abs(df_ref[...]))
    out_b = _dot3_tn(h_b, w3b_ref[...]) * jnp.exp(-t_b * dec_b)
    out_b = jnp.where(pos + 1.0 <= L - 1, out_b, 0.0)
    of_ref[...] = out_f.astype(of_ref.dtype)
    ob_ref[...] = out_b.astype(ob_ref.dtype)
    part = (jnp.sum(jnp.abs(out_f), axis=0, keepdims=True)
            + jnp.sum(jnp.abs(out_b), axis=0, keepdims=True))

    @pl.when(i == 0)
    def _():
        b0 = _dot3_tn(h[:, :LANES], w3b_ref[...])
        nrm_ref[...] = part + jnp.abs(b0[0:1])

    @pl.when(i > 0)
    def _():
        nrm_ref[...] += part


def _filter_mlp(L, w1, b1, f1, w2, b2, f2, w3, decay, tr=256):
    nf = 2 * D_HYENA
    w3r = w3.reshape(FILTER_HIDDEN, 2, 2, D_HYENA)
    dr = decay.reshape(2, 2, D_HYENA)
    w3f, w3b = w3r[:, :, 0].reshape(FILTER_HIDDEN, nf), w3r[:, :, 1].reshape(FILTER_HIDDEN, nf)
    df, db = dr[:, 0].reshape(1, nf), dr[:, 1].reshape(1, nf)
    bands = jnp.linspace(1e-4, FILTER_BANDS - 1, FILTER_BANDS, dtype=F32).reshape(FILTER_BANDS, 1)
    H = FILTER_HIDDEN
    col = lambda v: v.reshape(H, 1)
    full = lambda shape: pl.BlockSpec(shape, lambda i: (0,) * len(shape))
    return pl.pallas_call(
        functools.partial(_fmlp_kernel, L=L, tr=tr),
        grid=(L // tr,),
        in_specs=[full((FILTER_BANDS, 1)), full((H, 1)), full((H, FILTER_BANDS)),
                  full((H, FILTER_BANDS)), full((H, 1)), full((H, 1)), full((H, H)),
                  full((H, 1)), full((H, 1)), full((H, nf)), full((H, nf)),
                  full((1, nf)), full((1, nf))],
        out_specs=[pl.BlockSpec((tr, nf), lambda i: (i, 0)),
                   pl.BlockSpec((tr, nf), lambda i: (i, 0)),
                   pl.BlockSpec((1, nf), lambda i: (0, 0))],
        out_shape=[jax.ShapeDtypeStruct((L, nf), BF16), jax.ShapeDtypeStruct((L, nf), BF16),
                   jax.ShapeDtypeStruct((1, nf), F32)],
        compiler_params=_params("arbitrary"),
        name="filter_mlp",
    )(bands, w1[0:1].T, w1[1:1 + FILTER_BANDS].T, w1[1 + FILTER_BANDS:].T, col(b1), col(f1),
      w2.T, col(b2), col(f2), w3f, w3b, df, db)


def _cs(num, den):
    ang = 2.0 * np.pi * (np.asarray(num, np.int64) % den).astype(np.float64) / den
    return np.cos(ang), np.sin(ang)


def _dft_constants(L, pairs):
    n = 2 * L
    n1 = 128 if L >= 8192 else 64
    n2 = n // n1
    n1h = n1 // 2
    q = n1h if pairs else n1 // 4
    ar = np.arange
    eye = np.eye(SUB)
    bf = lambda m: jnp.asarray(m.astype(np.float32)).astype(BF16)

    c, s = _cs(ar(n1)[:, None] * ar(q)[None, :], n1)
    f1 = np.stack([np.concatenate([c, s], axis=1), np.concatenate([-s, c], axis=1)], axis=1)
    k1 = np.kron(f1.reshape(2 * n1, 2 * q), eye)

    c, s = _cs(ar(n1)[:, None] * ar(n1h)[None, :], n1)
    k1f = np.kron(np.stack([c, -s], axis=1).reshape(2 * n1, n1h), eye)
    c, s = _cs(ar(n1)[:, None] * (n1 - 1 - ar(n1h))[None, :], n1)
    k1b = np.kron(np.stack([c, -s], axis=1).reshape(2 * n1, n1h), eye[::-1])

    c2, s2 = (jnp.asarray(x.astype(np.float32)) for x in _cs(ar(n2)[:, None] * ar(n2)[None, :], n2))
    ct, st = (jnp.asarray(x.astype(np.float32)) for x in _cs(ar(n1)[:, None] * ar(n2)[None, :], n))
    gr = c2[None] * ct[:, None, :] - s2[None] * st[:, None, :]
    gi = -(s2[None] * ct[:, None, :] + c2[None] * st[:, None, :])
    gs = jnp.concatenate([jnp.concatenate([gr, -gi], axis=2),
                          jnp.concatenate([gi, gr], axis=2)], axis=1).astype(BF16)

    c, s = _cs(ar(n1h)[:, None] * ar(n1)[None, :], n1)
    if pairs:
        top = np.stack([c, -s], axis=2).reshape(n1h, 2 * n1)
        bot = np.stack([s, c], axis=2).reshape(n1h, 2 * n1)
        f3 = np.concatenate([top, bot], axis=0)
    else:
        c2h, s2h = _cs((ar(n1h)[:, None] - q) * ar(n1)[None, :], n1)
        f3 = np.stack([c + s2h, -s + c2h], axis=2).reshape(n1h, 2 * n1)
    k3 = np.kron(f3 / n, eye)
    return dict(n1=n1, n2=n2, q=q, k1=bf(k1), k1f=bf(k1f), k1b=bf(k1b), gs=gs, k3=bf(k3))


def _halves(x):
    cb = x.shape[-1]
    return [x[..., h * SUB:(h + 1) * SUB, :].reshape(-1, cb).astype(BF16) for h in range(2)]


def _join(lo, hi, lead):
    cb = lo.shape[-1]
    return jnp.concatenate([lo.reshape(*lead, SUB, cb), hi.reshape(*lead, SUB, cb)], axis=len(lead))


INNER_LANES = 256


def _inner_forward(scr_ref, k1_idx, c0, gs):
    n2 = scr_ref.shape[2]
    a = scr_ref[k1_idx, :, :, c0:c0 + INNER_LANES].reshape(2 * n2, INNER_LANES)
    return jnp.dot(gs, a, preferred_element_type=F32)


def _conv_kernel(u_ref, g_ref, d_ref, k1_ref, k3_ref, gs_ref, kf_ref, o_ref, scr_ref,
                 *, s1, s2):
    n1, _, n2, cb = scr_ref.shape
    kb = gs_ref.shape[0]
    q = u_ref.shape[1]
    s = pl.program_id(2)

    @pl.when(s < s1)
    def _():
        for p0 in range(0, GROUP, PACK):
            x = u_ref[:, :, p0:p0 + PACK, :].astype(F32)
            lo, hi = [jnp.dot(k1_ref[...], xh, preferred_element_type=F32) for xh in _halves(x)]
            row0 = pl.multiple_of(s * GROUP + p0, PACK)
            scr_ref[:, :, pl.ds(row0, PACK), :] = _join(lo, hi, (n1, 2)).astype(BF16)

    @pl.when((s >= s1) & (s < s1 + s2))
    def _():
        for kk in range(kb):
            k1_idx = (s - s1) * kb + kk
            for c0 in range(0, cb, INNER_LANES):
                u = _inner_forward(scr_ref, k1_idx, c0, gs_ref[kk])
                ur, ui = u[:n2], u[n2:]
                kr = kf_ref[kk, 0, :, c0:c0 + INNER_LANES].astype(F32)
                ki = kf_ref[kk, 1, :, c0:c0 + INNER_LANES].astype(F32)
                v = jnp.concatenate([ur * kr - ui * ki, ur * ki + ui * kr], axis=0).astype(BF16)
                b = jax.lax.dot_general(gs_ref[kk], v, (((0,), (0,)), ((), ())),
                                        preferred_element_type=F32)
                scr_ref[k1_idx, :, :, c0:c0 + INNER_LANES] = (
                    b.reshape(2, n2, INNER_LANES).astype(BF16))

    @pl.when(s >= s1 + s2)
    def _():
        for p0 in range(0, GROUP, PACK):
            row0 = pl.multiple_of((s - s1 - s2) * GROUP + p0, PACK)
            b = scr_ref[:, :, pl.ds(row0, PACK), :].astype(F32)
            lo, hi = [jnp.dot(k3_ref[...], x, preferred_element_type=F32) for x in _halves(b)]
            y = _join(lo, hi, (2, q))
            rows = slice(p0, p0 + PACK)
            z = g_ref[:, :, rows, :].astype(F32) * (y + u_ref[:, :, rows, :].astype(F32) * d_ref[...])
            o_ref[:, :, rows, :] = z.astype(o_ref.dtype)


def _long_conv(u6, u_off, g6, g_off, d, kf, order, dc, cb, kb):
    bp, _, q, n2p, _, _ = u6.shape
    C = D_HYENA
    n1, n2 = dc["n1"], dc["n2"]
    s1, s2 = n2p, n1 // kb
    koff = order * (C // cb)
    grp1 = lambda s: jnp.where(s < s1, s, jnp.where(s < s1 + s2, s1 - 1, s - s1 - s2))
    grp3 = lambda s: jnp.maximum(s - s1 - s2, 0)
    kblk = lambda s: jnp.clip(s - s1, 0, s2 - 1)
    tspec = lambda off, grp: pl.BlockSpec((None, 2, q, None, GROUP, cb),
                                          lambda b, c, s: (b, 0, 0, grp(s), 0, off + c))
    const = lambda a: pl.BlockSpec(a.shape, lambda b, c, s: (0,) * a.ndim,
                                   pipeline_mode=pl.Buffered(1))
    return pl.pallas_call(
        functools.partial(_conv_kernel, s1=s1, s2=s2),
        grid=(bp, C // cb, s1 + s2 + s1),
        in_specs=[tspec(u_off, grp1), tspec(g_off, grp3),
                  pl.BlockSpec((1, cb), lambda b, c, s: (0, c)),
                  const(dc["k1"]), const(dc["k3"]),
                  pl.BlockSpec((kb, 2 * n2, 2 * n2), lambda b, c, s: (kblk(s), 0, 0)),
                  pl.BlockSpec((kb, 2, n2, cb), lambda b, c, s: (kblk(s), 0, 0, koff + c))],
        out_specs=tspec(0, grp3),
        out_shape=jax.ShapeDtypeStruct((bp, 2, q, n2p, GROUP, C), BF16),
        scratch_shapes=[pltpu.VMEM((n1, 2, n2, cb), BF16)],
        compiler_params=_params("parallel", "parallel", "arbitrary"),
        name="long_conv",
    )(u6, g6, d.reshape(1, C), dc["k1"], dc["k3"], dc["gs"], kf)


def _hyena(hyc, kf, dc, hy_d, cb, kb):
    B, L, _ = hyc.shape
    C = D_HYENA
    n2, q = dc["n2"], dc["q"]
    bp = B * L // (2 * q * n2)
    hy6 = hyc.reshape(bp, 2, q, n2 // GROUP, GROUP, 3 * C)
    z6 = _long_conv(hy6, 0, hy6, C // cb, hy_d[0], kf, 0, dc, cb, kb)
    z6 = _long_conv(z6, 0, hy6, 2 * (C // cb), hy_d[1], kf, 1, dc, cb, kb)
    return z6.reshape(B, L, C)


def _filt_kernel(xf_ref, xb_ref, nrm_ref, k1f_ref, k1b_ref, gs_ref, o_ref, scr_ref, *, s1):
    n1, _, n2, cb = scr_ref.shape
    kb = gs_ref.shape[0]
    s = pl.program_id(1)

    @pl.when(s < s1)
    def _():
        d = lambda k, x: jnp.dot(k[...], x, preferred_element_type=F32)
        for p0 in range(0, GROUP, PACK):
            f_lo, f_hi = _halves(xf_ref[:, p0:p0 + PACK, :].astype(F32))
            b_lo, b_hi = _halves(xb_ref[:, GROUP - PACK - p0:GROUP - p0, :].astype(F32))
            lo = d(k1f_ref, f_lo) + d(k1b_ref, b_hi)
            hi = d(k1f_ref, f_hi) + d(k1b_ref, b_lo)
            row0 = pl.multiple_of(s * GROUP + p0, PACK)
            scr_ref[:, :, pl.ds(row0, PACK), :] = _join(lo, hi, (n1, 2)).astype(BF16)

    @pl.when(s >= s1)
    def _():
        for kk in range(kb):
            for c0 in range(0, cb, INNER_LANES):
                inv = 1.0 / nrm_ref[:, c0:c0 + INNER_LANES]
                u = _inner_forward(scr_ref, (s - s1) * kb + kk, c0, gs_ref[kk]) * inv
                o_ref[kk, :, :, c0:c0 + INNER_LANES] = (
                    u.reshape(2, n2, INNER_LANES).astype(o_ref.dtype))


def _filter_spectrum(L, dc, fw1, fb1, ff1, fw2, fb2, ff2, fw3, fdecay, cb, kb):
    hf, hb, nrm = _filter_mlp(L, fw1, fb1, ff1, fw2, fb2, ff2, fw3, fdecay)
    nf = hf.shape[1]
    n1, n2 = dc["n1"], dc["n2"]
    n1h, n2p = n1 // 2, n2 // GROUP
    s1, s2 = n2p, n1 // kb
    hf4 = hf.reshape(n1h, n2p, GROUP, nf)
    hb4 = hb.reshape(n1h, n2p, GROUP, nf)
    grp = lambda s: jnp.minimum(s, s1 - 1)
    kblk = lambda s: jnp.maximum(s - s1, 0)
    const = lambda a: pl.BlockSpec(a.shape, lambda c, s: (0,) * a.ndim,
                                   pipeline_mode=pl.Buffered(1))
    return pl.pallas_call(
        functools.partial(_filt_kernel, s1=s1),
        grid=(nf // cb, s1 + s2),
        in_specs=[pl.BlockSpec((n1h, None, GROUP, cb), lambda c, s: (0, grp(s), 0, c)),
                  pl.BlockSpec((n1h, None, GROUP, cb), lambda c, s: (0, s1 - 1 - grp(s), 0, c)),
                  pl.BlockSpec((1, cb), lambda c, s: (0, c)),
                  const(dc["k1f"]), const(dc["k1b"]),
                  pl.BlockSpec((kb, 2 * n2, 2 * n2), lambda c, s: (kblk(s), 0, 0))],
        out_specs=pl.BlockSpec((kb, 2, n2, cb), lambda c, s: (kblk(s), 0, 0, c)),
        out_shape=jax.ShapeDtypeStruct((n1, 2, n2, nf), BF16),
        scratch_shapes=[pltpu.VMEM((n1, 2, n2, cb), BF16)],
        compiler_params=_params("parallel", "arbitrary"),
        name="filter_spectrum",
    )(hf4, hb4, nrm, dc["k1f"], dc["k1b"], dc["gs"])


OUT_CHAINS = 2


def _out_kernel(a_ref, z_ref, gh_ref, hg_ref, wa_ref, wz_ref, h_ref, g_ref, b_ref, o_ref, ob_ref):
    chain = a_ref.shape[0] // OUT_CHAINS
    for r0 in range(0, a_ref.shape[0], chain):
        rows = slice(r0, r0 + chain)
        z = z_ref[rows, :].astype(F32)
        gh = gh_ref[rows, :].astype(F32)
        zn = z * jax.lax.rsqrt(jnp.mean(z * z, axis=-1, keepdims=True) + NORM_EPS) * hg_ref[...]
        zn = (zn * (gh * jax.nn.sigmoid(gh))).astype(BF16)
        acc = (jnp.dot(a_ref[rows, :], wa_ref[...], preferred_element_type=F32)
               + jnp.dot(zn, wz_ref[...], preferred_element_type=F32))
        y = DN_ALPHA * h_ref[rows, :] + acc
        mu = jnp.mean(y, axis=-1, keepdims=True)
        yc = y - mu
        var = jnp.mean(yc * yc, axis=-1, keepdims=True)
        out = yc * jax.lax.rsqrt(var + NORM_EPS) * g_ref[...] + b_ref[...]
        o_ref[rows, :] = out
        ob_ref[rows, :] = out.astype(BF16)


def _out_proj(a, z, proj, hy_g, w_out, h, ln_g, ln_b, tm=512):
    T, D = h.shape
    wa = w_out[:D_ATTN].astype(BF16)
    wz = w_out[D_ATTN:].astype(BF16)
    row = lambda w: pl.BlockSpec((tm, w), lambda i: (i, 0))
    const = lambda r, w: pl.BlockSpec((r, w), lambda i: (0, 0), pipeline_mode=pl.Buffered(1))
    return pl.pallas_call(
        _out_kernel,
        grid=(T // tm,),
        in_specs=[row(D_ATTN), row(D_HYENA),
                  pl.BlockSpec((tm, D_HYENA), lambda i: (i, COL_GH)),
                  const(1, D_HYENA), const(D_ATTN, D), const(D_HYENA, D), row(D),
                  const(1, D), const(1, D)],
        out_specs=[row(D), row(D)],
        out_shape=[jax.ShapeDtypeStruct((T, D), F32), jax.ShapeDtypeStruct((T, D), BF16)],
        compiler_params=_params("parallel"),
        name="out_proj",
    )(a, z, proj, hy_g.reshape(1, -1), wa, wz, h, ln_g.reshape(1, D), ln_b.reshape(1, D))


def _arrange_w_in(w):
    q, k, v, ga, hy, gh = jnp.split(w, [1024, 1280, 1536, 2560, 5632], axis=1)
    pair = _head_pair_lane_perm()
    pq = np.concatenate([pair + LANES * i for i in range(Q_COLS // LANES)])
    pk = np.concatenate([pair + LANES * i for i in range(KV_COLS // LANES)])
    plain = jnp.concatenate([q[:, pq], ga, gh, k[:, pk], v], axis=1).astype(BF16)
    return plain, hy.astype(BF16)


def _trunk(x, p):
    B, L, D = x.shape
    T = B * L
    pairs = B > 1
    dc = _dft_constants(L, pairs)
    cb, kb, kb_filt = (512, 32, 32) if L <= 2048 else (256, 8, 16)
    rope_tab = _rope_table(L)
    bias = _attn_bias()
    h, hb = _layernorm(x.reshape(T, D), p["emb_ln_g"], p["emb_ln_b"])
    for l in range(DEPTH):
        w_plain, w_hy = _arrange_w_in(p["w_in"][l])
        proj = _matmul(hb, w_plain, 1024, D_PLAIN // 2, BF16)
        proj3 = proj.reshape(B, L, D_PLAIN)
        a = _attention(proj3, rope_tab, bias, p["attn_sink"][l], p["attn_norm_g"][l])
        hyc = _proj_conv(hb, w_hy, p["conv_w"][l], p["conv_b"][l], L).reshape(B, L, 3 * D_HYENA)
        kf = _filter_spectrum(L, dc, p["flt_w1"][l], p["flt_b1"][l], p["flt_freq1"][l],
                              p["flt_w2"][l], p["flt_b2"][l], p["flt_freq2"][l],
                              p["flt_w3"][l], p["flt_decay"][l], cb, kb_filt)
        z = _hyena(hyc, kf, dc, p["hyena_d"][l], cb, kb)
        h, hb = _out_proj(a.reshape(T, D_ATTN), z.reshape(T, D_HYENA), proj,
                          p["hyena_norm_g"][l], p["w_out"][l], h, p["ln_g"][l], p["ln_b"][l])
    return h.reshape(B, L, D)


def kernel(x_prompt, x_sample, emb_ln_g, emb_ln_b, w_in, attn_sink, conv_w, conv_b, flt_w1,
           flt_b1, flt_freq1, flt_w2, flt_b2, flt_freq2, flt_w3, flt_decay, hyena_d,
           attn_norm_g, hyena_norm_g, w_out, ln_g, ln_b):
    p = dict(emb_ln_g=emb_ln_g, emb_ln_b=emb_ln_b, w_in=w_in, attn_sink=attn_sink,
             conv_w=conv_w, conv_b=conv_b, flt_w1=flt_w1, flt_b1=flt_b1, flt_freq1=flt_freq1,
             flt_w2=flt_w2, flt_b2=flt_b2, flt_freq2=flt_freq2, flt_w3=flt_w3,
             flt_decay=flt_decay, hyena_d=hyena_d, attn_norm_g=attn_norm_g,
             hyena_norm_g=hyena_norm_g, w_out=w_out, ln_g=ln_g, ln_b=ln_b)
    return (_trunk(x_prompt, p), _trunk(x_sample, p))
```

```python
import functools
import math

import numpy as np
import jax
import jax.numpy as jnp
from jax.experimental import pallas as pl
from jax.experimental.pallas import tpu as pltpu

F32 = jnp.float32
BF16 = jnp.bfloat16

D_MODEL = 2048
DEPTH = 2
D_ATTN = 1024
D_HYENA = 1024
HEAD_DIM = 64
N_HEADS = 16
N_KV_HEADS = 4
GQA_GROUPS = 4
ROT_DIM = 16
ROPE_THETA = 500000.0
WINDOW = 128
BLOCK = 128
FILTER_BANDS = 16
FILTER_HIDDEN = 64
DN_ALPHA = (2.0 * DEPTH) ** 0.25
NORM_EPS = 1e-5
MASK_VALUE = -1e30
LOG2E = math.log2(math.e)
Q_COLS = N_HEADS * HEAD_DIM
KV_COLS = N_KV_HEADS * HEAD_DIM
D_IN = 2 * Q_COLS + 2 * KV_COLS + 4 * D_HYENA

COL_Q, COL_GA, COL_GH = 0, 1, 2
COL_K, COL_V = 12, 13
D_PLAIN = 3 * 1024 + 2 * KV_COLS

LANES = 128
SUB = 8
PACK = 16
GROUP = 2 * PACK
VMEM_LIMIT = 48 * 1024 * 1024


def _params(*sem):
    return pltpu.CompilerParams(dimension_semantics=sem, vmem_limit_bytes=VMEM_LIMIT)


def _ln_kernel(x_ref, g_ref, b_ref, o_ref, ob_ref):
    x = x_ref[...]
    mu = jnp.mean(x, axis=-1, keepdims=True)
    xc = x - mu
    var = jnp.mean(xc * xc, axis=-1, keepdims=True)
    y = xc * jax.lax.rsqrt(var + NORM_EPS) * g_ref[...] + b_ref[...]
    o_ref[...] = y
    ob_ref[...] = y.astype(BF16)


def _layernorm(x, g, b, tm=512):
    T, D = x.shape
    return pl.pallas_call(
        _ln_kernel,
        grid=(T // tm,),
        in_specs=[pl.BlockSpec((tm, D), lambda i: (i, 0)),
                  pl.BlockSpec((1, D), lambda i: (0, 0)),
                  pl.BlockSpec((1, D), lambda i: (0, 0))],
        out_specs=[pl.BlockSpec((tm, D), lambda i: (i, 0)),
                   pl.BlockSpec((tm, D), lambda i: (i, 0))],
        out_shape=[jax.ShapeDtypeStruct((T, D), F32), jax.ShapeDtypeStruct((T, D), BF16)],
        compiler_params=_params("parallel"),
        name="layernorm",
    )(x, g.reshape(1, D), b.reshape(1, D))


def _mm_kernel(x_ref, w_ref, o_ref):
    o_ref[...] = jnp.dot(x_ref[...], w_ref[...], preferred_element_type=F32).astype(o_ref.dtype)


def _matmul(x, w, tm, tn, out_dtype):
    M, K = x.shape
    N = w.shape[1]
    return pl.pallas_call(
        _mm_kernel,
        grid=(M // tm, N // tn),
        in_specs=[pl.BlockSpec((tm, K), lambda i, j: (i, 0)),
                  pl.BlockSpec((K, tn), lambda i, j: (0, j))],
        out_specs=pl.BlockSpec((tm, tn), lambda i, j: (i, j)),
        out_shape=jax.ShapeDtypeStruct((M, N), out_dtype),
        compiler_params=_params("parallel", "parallel"),
        name="in_proj",
    )(x, w)


_SLOT_OF_OCTET = (0, 4, 1, 2, 3, 5, 6, 7)


def _rope(t, tab):
    return t * tab[0] + pltpu.roll(t, LANES // 2, 1) * tab[1]


def _attn_kernel(sink_ref, q_ref, ga_ref, kp_ref, kc_ref, kn_ref, vp_ref, vc_ref, vn_ref,
                 tp_ref, tc_ref, tn_ref, bias_ref, g_ref, o_ref, acc_ref):
    nkeys = 3 * BLOCK
    tabs = (tp_ref[...], tc_ref[...], tn_ref[...])
    krefs = (kp_ref, kc_ref, kn_ref)
    v3 = jnp.concatenate([vp_ref[...], vc_ref[...], vn_ref[...]], axis=0)
    lane = jax.lax.broadcasted_iota(jnp.int32, (1, LANES), 1)
    head_a = (lane % 16) < 8
    first_pair = jax.lax.broadcasted_iota(jnp.int32, (1, 2 * BLOCK), 1) < BLOCK
    scale = HEAD_DIM ** -0.5 * LOG2E
    kcols = [jnp.concatenate(
        [_rope(krefs[w][:, c * LANES:(c + 1) * LANES].astype(F32), tabs[w]) for w in range(3)],
        axis=0) for c in range(KV_COLS // LANES)]

    for g in range(N_KV_HEADS):
        kcol = kcols[g // 2]
        if g % 2 == 0:
            k_a = jnp.where(head_a, kcol, 0.0)
            k_b = pltpu.roll(k_a, 8, 1)
        else:
            k_b = jnp.where(head_a, 0.0, kcol)
            k_a = pltpu.roll(k_b, LANES - 8, 1)
        kst = jnp.concatenate([k_a, k_b], axis=0).astype(BF16)
        q2 = jnp.concatenate(
            [(_rope(q_ref[:, (2 * g + pr) * LANES:(2 * g + pr + 1) * LANES].astype(F32), tabs[1])
              * scale).astype(BF16) for pr in range(2)], axis=0)
        st = jax.lax.dot_general(kst, q2, (((1,), (1,)), ((), ())), preferred_element_type=F32)
        st = st + bias_ref[...]
        vg = v3[:, g * HEAD_DIM:(g + 1) * HEAD_DIM]
        for par in range(2):
            for pr in range(2):
                h = 4 * g + 2 * pr + par
                s = st[par * nkeys:(par + 1) * nkeys, pr * BLOCK:(pr + 1) * BLOCK]
                sk = sink_ref[h] * LOG2E
                m = jnp.maximum(jnp.max(s, axis=0, keepdims=True), sk)
                p = jnp.exp2(s - m)
                denom = jnp.sum(p, axis=0, keepdims=True) + jnp.exp2(sk - m)
                pn = (p * (1.0 / denom)).astype(BF16)
                o = jax.lax.dot_general(pn, vg, (((0,), (0,)), ((), ())),
                                        preferred_element_type=F32)
                acc_ref[:, h * HEAD_DIM:(h + 1) * HEAD_DIM] = o

    a = acc_ref[...]
    an = a * jax.lax.rsqrt(jnp.mean(a * a, axis=-1, keepdims=True) + NORM_EPS) * g_ref[...]
    ga = ga_ref[...].astype(F32)
    o_ref[...] = (an * (ga * jax.nn.sigmoid(ga))).astype(o_ref.dtype)


def _attention(proj3, rope_tab, bias, sink, attn_g):
    B, L, _ = proj3.shape
    nb = L // BLOCK
    assert nb >= 2
    prev = lambda n: jnp.maximum(n - 1, 0)
    nxt = lambda n: jnp.minimum(n + 1, nb - 1)
    which = (prev, lambda n: n, nxt)

    def kv_spec(col, w):
        return pl.BlockSpec((None, BLOCK, KV_COLS), lambda b, n: (b, which[w](n), col))

    def tab_spec(w):
        return pl.BlockSpec((2, BLOCK, LANES), lambda b, n: (0, which[w](n), 0))

    return pl.pallas_call(
        _attn_kernel,
        grid=(B, nb),
        in_specs=[pl.BlockSpec(memory_space=pltpu.SMEM),
                  pl.BlockSpec((None, BLOCK, Q_COLS), lambda b, n: (b, n, COL_Q)),
                  pl.BlockSpec((None, BLOCK, D_ATTN), lambda b, n: (b, n, COL_GA)),
                  kv_spec(COL_K, 0), kv_spec(COL_K, 1), kv_spec(COL_K, 2),
                  kv_spec(COL_V, 0), kv_spec(COL_V, 1), kv_spec(COL_V, 2),
                  tab_spec(0), tab_spec(1), tab_spec(2),
                  pl.BlockSpec((None, 6 * BLOCK, 2 * BLOCK),
                               lambda b, n: (jnp.where(n == 0, 0, jnp.where(n == nb - 1, 2, 1)), 0, 0)),
                  pl.BlockSpec((1, D_ATTN), lambda b, n: (0, 0))],
        out_specs=pl.BlockSpec((None, BLOCK, D_ATTN), lambda b, n: (b, n, 0)),
        out_shape=jax.ShapeDtypeStruct((B, L, D_ATTN), BF16),
        scratch_shapes=[pltpu.VMEM((BLOCK, D_ATTN), F32)],
        compiler_params=_params("parallel", "parallel"),
        name="window_attention",
    )(sink, proj3, proj3, proj3, proj3, proj3, proj3, proj3, proj3,
      rope_tab, rope_tab, rope_tab, bias, attn_g.reshape(1, D_ATTN))


def _rope_table(L):
    inv = ROPE_THETA ** (-jnp.arange(0, ROT_DIM, 2, dtype=F32) / ROT_DIM)
    lane = np.arange(LANES)
    slot = lane // 16
    rotary = (slot == 0) | (slot == 4)
    inv_lane = jnp.where(jnp.asarray(rotary), inv[lane % 8], 0.0)
    sign = jnp.asarray(np.where(slot == 0, -1.0, np.where(slot == 4, 1.0, 0.0)).astype(np.float32))
    ang = jnp.arange(L, dtype=F32)[:, None] * inv_lane[None, :]
    return jnp.stack([jnp.cos(ang), jnp.sin(ang) * sign[None, :]])


def _attn_bias():
    c = np.arange(3 * BLOCK)[:, None]
    r = np.arange(BLOCK)[None, :]
    band = (c >= r) & (c <= r + 2 * WINDOW)
    variants = [band & (c >= BLOCK), band, band & (c < 2 * BLOCK)]
    out = np.stack([np.tile(np.where(v, 0.0, MASK_VALUE), (2, 2)) for v in variants])
    return jnp.asarray(out.astype(np.float32))


HALO = PACK


def _proj_conv_kernel(xp_ref, x_ref, xn_ref, w_ref, cw_ref, cb_ref, o_ref, res_ref, *, L, chunk):
    tm = x_ref.shape[0]
    i = pl.program_id(0)
    xx = jnp.concatenate([xp_ref[...], x_ref[...], xn_ref[...]], axis=0)
    res_ref[...] = jnp.dot(xx, w_ref[...], preferred_element_type=F32)
    before = res_ref[HALO - SUB:HALO, :]
    res_ref[HALO - SUB:HALO, :] = jnp.where((i * tm) % L == 0, 0.0, before)
    after = res_ref[HALO + tm:HALO + tm + SUB, :]
    res_ref[HALO + tm:HALO + tm + SUB, :] = jnp.where(((i + 1) * tm) % L == 0, 0.0, after)

    w0, w1, w2, b = cw_ref[0:1, :], cw_ref[1:2, :], cw_ref[2:3, :], cb_ref[...]
    for r0 in range(0, tm, chunk):
        xm = res_ref[HALO - 1 + r0:HALO - 1 + r0 + chunk, :]
        x0 = res_ref[HALO + r0:HALO + r0 + chunk, :]
        xq = res_ref[HALO + 1 + r0:HALO + 1 + r0 + chunk, :]
        o_ref[r0:r0 + chunk, :] = (xm * w0 + x0 * w1 + xq * w2 + b).astype(o_ref.dtype)


def _proj_conv(x, w, conv_w, conv_b, L, tm=1024, tn=1024):
    T, K = x.shape
    N = w.shape[1]
    assert L % tm == 0 and tm % HALO == 0
    nh = T // HALO
    per = tm // HALO
    return pl.pallas_call(
        functools.partial(_proj_conv_kernel, L=L, chunk=256),
        grid=(T // tm, N // tn),
        in_specs=[pl.BlockSpec((HALO, K), lambda i, j: (jnp.maximum(i * per - 1, 0), 0)),
                  pl.BlockSpec((tm, K), lambda i, j: (i, 0)),
                  pl.BlockSpec((HALO, K), lambda i, j: (jnp.minimum((i + 1) * per, nh - 1), 0)),
                  pl.BlockSpec((K, tn), lambda i, j: (0, j)),
                  pl.BlockSpec((3, tn), lambda i, j: (0, j)),
                  pl.BlockSpec((1, tn), lambda i, j: (0, j))],
        out_specs=pl.BlockSpec((tm, tn), lambda i, j: (i, j)),
        out_shape=jax.ShapeDtypeStruct((T, N), BF16),
        scratch_shapes=[pltpu.VMEM((tm + 2 * HALO, tn), F32)],
        compiler_params=_params("parallel", "parallel"),
        name="in_proj_conv",
    )(x, x, x, w, conv_w, conv_b.reshape(1, -1))


def _split(a):
    hi = a.astype(BF16)
    return hi, (a - hi.astype(F32)).astype(BF16)


def _dot3(a, b):
    ah, al = _split(a)
    bh, bl = _split(b)
    d = lambda x, y: jnp.dot(x, y, preferred_element_type=F32)
    return d(ah, bh) + (d(ah, bl) + d(al, bh))


def _dot3_tn(at, b):
    ah, al = _split(at)
    bh, bl = _split(b)
    a3 = jnp.concatenate([ah, al, ah], axis=0)
    b3 = jnp.concatenate([bh, bh, bl], axis=0)
    return jax.lax.dot_general(a3, b3, (((0,), (0,)), ((), ())), preferred_element_type=F32)


def _fmlp_kernel(bands_ref, w1t_ref, w1c_ref, w1s_ref, b1_ref, f1_ref, w2_ref, b2_ref, f2_ref,
                 w3f_ref, w3b_ref, df_ref, db_ref, of_ref, ob_ref, nrm_ref, *, L, tr):
    i = pl.program_id(0)
    n = tr + LANES
    lag = (i * tr + jax.lax.broadcasted_iota(jnp.int32, (1, n), 1)).astype(F32)
    t_row = lag / (L - 1)
    ang = bands_ref[...] * (2.0 * math.pi * lag / L)
    pre = (w1t_ref[...] * t_row + _dot3(w1c_ref[...], jnp.cos(ang))
           + _dot3(w1s_ref[...], -jnp.sin(ang)) + b1_ref[...])
    h = jnp.sin(f1_ref[...] * pre)
    h = jnp.sin(f2_ref[...] * (_dot3(w2_ref[...], h) + b2_ref[...]))

    h_a = h[:, :tr]
    h_b = pltpu.roll(h, n - 1, 1)[:, :tr]
    pos = (i * tr + jax.lax.broadcasted_iota(jnp.int32, (tr, 1), 0)).astype(F32)
    t_a = pos / (L - 1)
    t_b = (pos + 1.0) / (L - 1)
    dec_b = jnp.abs(db_ref[...])
    out_f = _dot3_tn(h_a, w3f_ref[...]) * jnp.exp(-t_a * jnp.abs(df_ref[...]))
    out_b = _dot3_tn(h_b, w3b_ref[...]) * jnp.exp(-t_b * dec_b)
    out_b = jnp.where(pos + 1.0 <= L - 1, out_b, 0.0)
    of_ref[...] = out_f.astype(of_ref.dtype)
    ob_ref[...] = out_b.astype(ob_ref.dtype)
    part = (jnp.sum(jnp.abs(out_f), axis=0, keepdims=True)
            + jnp.sum(jnp.abs(out_b), axis=0, keepdims=True))

    @pl.when(i == 0)
    def _():
        b0 = _dot3_tn(h[:, :LANES], w3b_ref[...])
        nrm_ref[...] = part + jnp.abs(b0[0:1])

    @pl.when(i > 0)
    def _():
        nrm_ref[...] += part


def _filter_mlp(L, w1, b1, f1, w2, b2, f2, w3, decay, tr=256):
    nf = 2 * D_HYENA
    w3r = w3.reshape(FILTER_HIDDEN, 2, 2, D_HYENA)
    dr = decay.reshape(2, 2, D_HYENA)
    w3f, w3b = w3r[:, :, 0].reshape(FILTER_HIDDEN, nf), w3r[:, :, 1].reshape(FILTER_HIDDEN, nf)
    df, db = dr[:, 0].reshape(1, nf), dr[:, 1].reshape(1, nf)
    bands = jnp.linspace(1e-4, FILTER_BANDS - 1, FILTER_BANDS, dtype=F32).reshape(FILTER_BANDS, 1)
    H = FILTER_HIDDEN
    col = lambda v: v.reshape(H, 1)
    full = lambda shape: pl.BlockSpec(shape, lambda i: (0,) * len(shape))
    return pl.pallas_call(
        functools.partial(_fmlp_kernel, L=L, tr=tr),
        grid=(L // tr,),
        in_specs=[full((FILTER_BANDS, 1)), full((H, 1)), full((H, FILTER_BANDS)),
                  full((H, FILTER_BANDS)), full((H, 1)), full((H, 1)), full((H, H)),
                  full((H, 1)), full((H, 1)), full((H, nf)), full((H, nf)),
                  full((1, nf)), full((1, nf))],
        out_specs=[pl.BlockSpec((tr, nf), lambda i: (i, 0)),
                   pl.BlockSpec((tr, nf), lambda i: (i, 0)),
                   pl.BlockSpec((1, nf), lambda i: (0, 0))],
        out_shape=[jax.ShapeDtypeStruct((L, nf), BF16), jax.ShapeDtypeStruct((L, nf), BF16),
                   jax.ShapeDtypeStruct((1, nf), F32)],
        compiler_params=_params("arbitrary"),
        name="filter_mlp",
    )(bands, w1[0:1].T, w1[1:1 + FILTER_BANDS].T, w1[1 + FILTER_BANDS:].T, col(b1), col(f1),
      w2.T, col(b2), col(f2), w3f, w3b, df, db)


def _cs(num, den):
    ang = 2.0 * np.pi * (np.asarray(num, np.int64) % den).astype(np.float64) / den
    return np.cos(ang), np.sin(ang)


def _dft_constants(L, pairs):
    n = 2 * L
    n1 = 128 if L >= 8192 else 64
    n2 = n // n1
    n1h = n1 // 2
    q = n1h if pairs else n1 // 4
    ar = np.arange
    eye = np.eye(SUB)
    bf = lambda m: jnp.asarray(m.astype(np.float32)).astype(BF16)

    c, s = _cs(ar(n1)[:, None] * ar(q)[None, :], n1)
    f1 = np.stack([np.concatenate([c, s], axis=1), np.concatenate([-s, c], axis=1)], axis=1)
    k1 = np.kron(f1.reshape(2 * n1, 2 * q), eye)

    c, s = _cs(ar(n1)[:, None] * ar(n1h)[None, :], n1)
    k1f = np.kron(np.stack([c, -s], axis=1).reshape(2 * n1, n1h), eye)
    c, s = _cs(ar(n1)[:, None] * (n1 - 1 - ar(n1h))[None, :], n1)
    k1b = np.kron(np.stack([c, -s], axis=1).reshape(2 * n1, n1h), eye[::-1])

    c2, s2 = (jnp.asarray(x.astype(np.float32)) for x in _cs(ar(n2)[:, None] * ar(n2)[None, :], n2))
    ct, st = (jnp.asarray(x.astype(np.float32)) for x in _cs(ar(n1)[:, None] * ar(n2)[None, :], n))
    gr = c2[None] * ct[:, None, :] - s2[None] * st[:, None, :]
    gi = -(s2[None] * ct[:, None, :] + c2[None] * st[:, None, :])
    gs = jnp.concatenate([jnp.concatenate([gr, -gi], axis=2),
                          jnp.concatenate([gi, gr], axis=2)], axis=1).astype(BF16)

    c, s = _cs(ar(n1h)[:, None] * ar(n1)[None, :], n1)
    if pairs:
        top = np.stack([c, -s], axis=2).reshape(n1h, 2 * n1)
        bot = np.stack([s, c], axis=2).reshape(n1h, 2 * n1)
        f3 = np.concatenate([top, bot], axis=0)
    else:
        c2h, s2h = _cs((ar(n1h)[:, None] - q) * ar(n1)[None, :], n1)
        f3 = np.stack([c + s2h, -s + c2h], axis=2).reshape(n1h, 2 * n1)
    k3 = np.kron(f3 / n, eye)
    return dict(n1=n1, n2=n2, q=q, k1=bf(k1), k1f=bf(k1f), k1b=bf(k1b), gs=gs, k3=bf(k3))


def _halves(x):
    cb = x.shape[-1]
    return [x[..., h * SUB:(h + 1) * SUB, :].reshape(-1, cb).astype(BF16) for h in range(2)]


def _join(lo, hi, lead):
    cb = lo.shape[-1]
    return jnp.concatenate([lo.reshape(*lead, SUB, cb), hi.reshape(*lead, SUB, cb)], axis=len(lead))


INNER_LANES = 256


def _inner_forward(scr_ref, k1_idx, c0, gs):
    n2 = scr_ref.shape[2]
    a = scr_ref[k1_idx, :, :, c0:c0 + INNER_LANES].reshape(2 * n2, INNER_LANES)
    return jnp.dot(gs, a, preferred_element_type=F32)


def _conv_kernel(u_ref, g_ref, d_ref, k1_ref, k3_ref, gs_ref, kf_ref, o_ref, scr_ref,
                 *, s1, s2):
    n1, _, n2, cb = scr_ref.shape
    kb = gs_ref.shape[0]
    q = u_ref.shape[1]
    s = pl.program_id(2)

    @pl.when(s < s1)
    def _():
        for p0 in range(0, GROUP, PACK):
            x = u_ref[:, :, p0:p0 + PACK, :].astype(F32)
            lo, hi = [jnp.dot(k1_ref[...], xh, preferred_element_type=F32) for xh in _halves(x)]
            row0 = pl.multiple_of(s * GROUP + p0, PACK)
            scr_ref[:, :, pl.ds(row0, PACK), :] = _join(lo, hi, (n1, 2)).astype(BF16)

    @pl.when((s >= s1) & (s < s1 + s2))
    def _():
        for kk in range(kb):
            k1_idx = (s - s1) * kb + kk
            for c0 in range(0, cb, INNER_LANES):
                u = _inner_forward(scr_ref, k1_idx, c0, gs_ref[kk])
                ur, ui = u[:n2], u[n2:]
                kr = kf_ref[kk, 0, :, c0:c0 + INNER_LANES].astype(F32)
                ki = kf_ref[kk, 1, :, c0:c0 + INNER_LANES].astype(F32)
                v = jnp.concatenate([ur * kr - ui * ki, ur * ki + ui * kr], axis=0).astype(BF16)
                b = jax.lax.dot_general(gs_ref[kk], v, (((0,), (0,)), ((), ())),
                                        preferred_element_type=F32)
                scr_ref[k1_idx, :, :, c0:c0 + INNER_LANES] = (
                    b.reshape(2, n2, INNER_LANES).astype(BF16))

    @pl.when(s >= s1 + s2)
    def _():
        for p0 in range(0, GROUP, PACK):
            row0 = pl.multiple_of((s - s1 - s2) * GROUP + p0, PACK)
            b = scr_ref[:, :, pl.ds(row0, PACK), :].astype(F32)
            lo, hi = [jnp.dot(k3_ref[...], x, preferred_element_type=F32) for x in _halves(b)]
            y = _join(lo, hi, (2, q))
            rows = slice(p0, p0 + PACK)
            z = g_ref[:, :, rows, :].astype(F32) * (y + u_ref[:, :, rows, :].astype(F32) * d_ref[...])
            o_ref[:, :, rows, :] = z.astype(o_ref.dtype)


def _long_conv(u6, u_off, g6, g_off, d, kf, order, dc, cb, kb):
    bp, _, q, n2p, _, _ = u6.shape
    C = D_HYENA
    n1, n2 = dc["n1"], dc["n2"]
    s1, s2 = n2p, n1 // kb
    koff = order * (C // cb)
    grp1 = lambda s: jnp.where(s < s1, s, jnp.where(s < s1 + s2, s1 - 1, s - s1 - s2))
    grp3 = lambda s: jnp.maximum(s - s1 - s2, 0)
    kblk = lambda s: jnp.clip(s - s1, 0, s2 - 1)
    tspec = lambda off, grp: pl.BlockSpec((None, 2, q, None, GROUP, cb),
                                          lambda b, c, s: (b, 0, 0, grp(s), 0, off + c))
    const = lambda a: pl.BlockSpec(a.shape, lambda b, c, s: (0,) * a.ndim,
                                   pipeline_mode=pl.Buffered(1))
    return pl.pallas_call(
        functools.partial(_conv_kernel, s1=s1, s2=s2),
        grid=(bp, C // cb, s1 + s2 + s1),
        in_specs=[tspec(u_off, grp1), tspec(g_off, grp3),
                  pl.BlockSpec((1, cb), lambda b, c, s: (0, c)),
                  const(dc["k1"]), const(dc["k3"]),
                  pl.BlockSpec((kb, 2 * n2, 2 * n2), lambda b, c, s: (kblk(s), 0, 0)),
                  pl.BlockSpec((kb, 2, n2, cb), lambda b, c, s: (kblk(s), 0, 0, koff + c))],
        out_specs=tspec(0, grp3),
        out_shape=jax.ShapeDtypeStruct((bp, 2, q, n2p, GROUP, C), BF16),
        scratch_shapes=[pltpu.VMEM((n1, 2, n2, cb), BF16)],
        compiler_params=_params("parallel", "parallel", "arbitrary"),
        name="long_conv",
    )(u6, g6, d.reshape(1, C), dc["k1"], dc["k3"], dc["gs"], kf)


def _hyena(hyc, kf, dc, hy_d, cb, kb):
    B, L, _ = hyc.shape
    C = D_HYENA
    n2, q = dc["n2"], dc["q"]
    bp = B * L // (2 * q * n2)
    hy6 = hyc.reshape(bp, 2, q, n2 // GROUP, GROUP, 3 * C)
    z6 = _long_conv(hy6, 0, hy6, C // cb, hy_d[0], kf, 0, dc, cb, kb)
    z6 = _long_conv(z6, 0, hy6, 2 * (C // cb), hy_d[1], kf, 1, dc, cb, kb)
    return z6.reshape(B, L, C)


def _filt_kernel(xf_ref, xb_ref, nrm_ref, k1f_ref, k1b_ref, gs_ref, o_ref, scr_ref, *, s1):
    n1, _, n2, cb = scr_ref.shape
    kb = gs_ref.shape[0]
    s = pl.program_id(1)

    @pl.when(s < s1)
    def _():
        d = lambda k, x: jnp.dot(k[...], x, preferred_element_type=F32)
        for p0 in range(0, GROUP, PACK):
            f_lo, f_hi = _halves(xf_ref[:, p0:p0 + PACK, :].astype(F32))
            b_lo, b_hi = _halves(xb_ref[:, GROUP - PACK - p0:GROUP - p0, :].astype(F32))
            lo = d(k1f_ref, f_lo) + d(k1b_ref, b_hi)
            hi = d(k1f_ref, f_hi) + d(k1b_ref, b_lo)
            row0 = pl.multiple_of(s * GROUP + p0, PACK)
            scr_ref[:, :, pl.ds(row0, PACK), :] = _join(lo, hi, (n1, 2)).astype(BF16)

    @pl.when(s >= s1)
    def _():
        for kk in range(kb):
            for c0 in range(0, cb, INNER_LANES):
                inv = 1.0 / nrm_ref[:, c0:c0 + INNER_LANES]
                u = _inner_forward(scr_ref, (s - s1) * kb + kk, c0, gs_ref[kk]) * inv
                o_ref[kk, :, :, c0:c0 + INNER_LANES] = (
                    u.reshape(2, n2, INNER_LANES).astype(o_ref.dtype))


def _filter_spectrum(L, dc, fw1, fb1, ff1, fw2, fb2, ff2, fw3, fdecay, cb, kb):
    hf, hb, nrm = _filter_mlp(L, fw1, fb1, ff1, fw2, fb2, ff2, fw3, fdecay)
    nf = hf.shape[1]
    n1, n2 = dc["n1"], dc["n2"]
    n1h, n2p = n1 // 2, n2 // GROUP
    s1, s2 = n2p, n1 // kb
    hf4 = hf.reshape(n1h, n2p, GROUP, nf)
    hb4 = hb.reshape(n1h, n2p, GROUP, nf)
    grp = lambda s: jnp.minimum(s, s1 - 1)
    kblk = lambda s: jnp.maximum(s - s1, 0)
    const = lambda a: pl.BlockSpec(a.shape, lambda c, s: (0,) * a.ndim,
                                   pipeline_mode=pl.Buffered(1))
    return pl.pallas_call(
        functools.partial(_filt_kernel, s1=s1),
        grid=(nf // cb, s1 + s2),
        in_specs=[pl.BlockSpec((n1h, None, GROUP, cb), lambda c, s: (0, grp(s), 0, c)),
                  pl.BlockSpec((n1h, None, GROUP, cb), lambda c, s: (0, s1 - 1 - grp(s), 0, c)),
                  pl.BlockSpec((1, cb), lambda c, s: (0, c)),
                  const(dc["k1f"]), const(dc["k1b"]),
                  pl.BlockSpec((kb, 2 * n2, 2 * n2), lambda c, s: (kblk(s), 0, 0))],
        out_specs=pl.BlockSpec((kb, 2, n2, cb), lambda c, s: (kblk(s), 0, 0, c)),
        out_shape=jax.ShapeDtypeStruct((n1, 2, n2, nf), BF16),
        scratch_shapes=[pltpu.VMEM((n1, 2, n2, cb), BF16)],
        compiler_params=_params("parallel", "arbitrary"),
        name="filter_spectrum",
    )(hf4, hb4, nrm, dc["k1f"], dc["k1b"], dc["gs"])


OUT_CHAINS = 2


def _out_kernel(a_ref, z_ref, gh_ref, hg_ref, wa_ref, wz_ref, h_ref, g_ref, b_ref, o_ref, ob_ref):
    chain = a_ref.shape[0] // OUT_CHAINS
    for r0 in range(0, a_ref.shape[0], chain):
        rows = slice(r0, r0 + chain)
        z = z_ref[rows, :].astype(F32)
        gh = gh_ref[rows, :].astype(F32)
        zn = z * jax.lax.rsqrt(jnp.mean(z * z, axis=-1, keepdims=True) + NORM_EPS) * hg_ref[...]
        zn = (zn * (gh * jax.nn.sigmoid(gh))).astype(BF16)
        acc = (jnp.dot(a_ref[rows, :], wa_ref[...], preferred_element_type=F32)
               + jnp.dot(zn, wz_ref[...], preferred_element_type=F32))
        y = DN_ALPHA * h_ref[rows, :] + acc
        mu = jnp.mean(y, axis=-1, keepdims=True)
        yc = y - mu
        var = jnp.mean(yc * yc, axis=-1, keepdims=True)
        out = yc * jax.lax.rsqrt(var + NORM_EPS) * g_ref[...] + b_ref[...]
        o_ref[rows, :] = out
        ob_ref[rows, :] = out.astype(BF16)


def _out_proj(a, z, proj, hy_g, w_out, h, ln_g, ln_b, tm=512):
    T, D = h.shape
    wa = w_out[:D_ATTN].astype(BF16)
    wz = w_out[D_ATTN:].astype(BF16)
    row = lambda w: pl.BlockSpec((tm, w), lambda i: (i, 0))
    const = lambda r, w: pl.BlockSpec((r, w), lambda i: (0, 0), pipeline_mode=pl.Buffered(1))
    return pl.pallas_call(
        _out_kernel,
        grid=(T // tm,),
        in_specs=[row(D_ATTN), row(D_HYENA),
                  pl.BlockSpec((tm, D_HYENA), lambda i: (i, COL_GH)),
                  const(1, D_HYENA), const(D_ATTN, D), const(D_HYENA, D), row(D),
                  const(1, D), const(1, D)],
        out_specs=[row(D), row(D)],
        out_shape=[jax.ShapeDtypeStruct((T, D), F32), jax.ShapeDtypeStruct((T, D), BF16)],
        compiler_params=_params("parallel"),
        name="out_proj",
    )(a, z, proj, hy_g.reshape(1, -1), wa, wz, h, ln_g.reshape(1, D), ln_b.reshape(1, D))


def _pair_lane_order(w):
    rows = w.shape[0]
    w5 = w.reshape(rows, -1, 2, 8, 8)
    octet_of_slot = np.argsort(_SLOT_OF_OCTET)
    w5 = jnp.concatenate([w5[:, :, :, o:o + 1, :] for o in octet_of_slot], axis=3)
    return w5.transpose(0, 1, 3, 2, 4).reshape(rows, -1)


def _arrange_w_in(w):
    q, k, v, ga, hy, gh = jnp.split(w, [1024, 1280, 1536, 2560, 5632], axis=1)
    plain = jnp.concatenate([_pair_lane_order(q), ga, gh, _pair_lane_order(k), v], axis=1)
    return plain.astype(BF16), hy.astype(BF16)


def _trunk(x, p):
    B, L, D = x.shape
    T = B * L
    pairs = B > 1
    dc = _dft_constants(L, pairs)
    cb, kb, kb_filt = (512, 32, 32) if L <= 2048 else (256, 16, 16)
    rope_tab = _rope_table(L)
    bias = _attn_bias()
    h, hb = _layernorm(x.reshape(T, D), p["emb_ln_g"], p["emb_ln_b"])
    for l in range(DEPTH):
        w_plain, w_hy = _arrange_w_in(p["w_in"][l])
        proj = _matmul(hb, w_plain, 1024, D_PLAIN // 2, BF16)
        proj3 = proj.reshape(B, L, D_PLAIN)
        a = _attention(proj3, rope_tab, bias, p["attn_sink"][l], p["attn_norm_g"][l])
        hyc = _proj_conv(hb, w_hy, p["conv_w"][l], p["conv_b"][l], L).reshape(B, L, 3 * D_HYENA)
        kf = _filter_spectrum(L, dc, p["flt_w1"][l], p["flt_b1"][l], p["flt_freq1"][l],
                              p["flt_w2"][l], p["flt_b2"][l], p["flt_freq2"][l],
                              p["flt_w3"][l], p["flt_decay"][l], cb, kb_filt)
        z = _hyena(hyc, kf, dc, p["hyena_d"][l], cb, kb)
        h, hb = _out_proj(a.reshape(T, D_ATTN), z.reshape(T, D_HYENA), proj,
                          p["hyena_norm_g"][l], p["w_out"][l], h, p["ln_g"][l], p["ln_b"][l])
    return h.reshape(B, L, D)


def kernel(x_prompt, x_sample, emb_ln_g, emb_ln_b, w_in, attn_sink, conv_w, conv_b, flt_w1,
           flt_b1, flt_freq1, flt_w2, flt_b2, flt_freq2, flt_w3, flt_decay, hyena_d,
           attn_norm_g, hyena_norm_g, w_out, ln_g, ln_b):
    p = dict(emb_ln_g=emb_ln_g, emb_ln_b=emb_ln_b, w_in=w_in, attn_sink=attn_sink,
             conv_w=conv_w, conv_b=conv_b, flt_w1=flt_w1, flt_b1=flt_b1, flt_freq1=flt_freq1,
             flt_w2=flt_w2, flt_b2=flt_b2, flt_freq2=flt_freq2, flt_w3=flt_w3,
             flt_decay=flt_decay, hyena_d=hyena_d, attn_norm_g=attn_norm_g,
             hyena_norm_g=hyena_norm_g, w_out=w_out, ln_g=ln_g, ln_b=ln_b)
    return (_trunk(x_prompt, p), _trunk(x_sample, p))
```

```python
import functools
import math

import numpy as np
import jax
import jax.numpy as jnp
from jax.experimental import pallas as pl
from jax.experimental.pallas import tpu as pltpu

F32 = jnp.float32
BF16 = jnp.bfloat16

D_MODEL = 2048
DEPTH = 2
D_ATTN = 1024
D_HYENA = 1024
HEAD_DIM = 64
N_HEADS = 16
N_KV_HEADS = 4
GQA_GROUPS = 4
ROT_DIM = 16
ROPE_THETA = 500000.0
WINDOW = 128
BLOCK = 128
FILTER_BANDS = 16
FILTER_HIDDEN = 64
DN_ALPHA = (2.0 * DEPTH) ** 0.25
NORM_EPS = 1e-5
MASK_VALUE = -1e30
LOG2E = math.log2(math.e)
Q_COLS = N_HEADS * HEAD_DIM
KV_COLS = N_KV_HEADS * HEAD_DIM
D_IN = 2 * Q_COLS + 2 * KV_COLS + 4 * D_HYENA

COL_Q, COL_GA, COL_GH = 0, 1, 2
COL_K, COL_V = 12, 13
D_PLAIN = 3 * 1024 + 2 * KV_COLS

LANES = 128
SUB = 8
PACK = 16
GROUP = 2 * PACK
V7X_VMEM_BYTES = 64 * 1024 * 1024
VMEM_LIMIT = V7X_VMEM_BYTES * 3 // 4


def _params(*sem):
    return pltpu.CompilerParams(dimension_semantics=sem, vmem_limit_bytes=VMEM_LIMIT)


def _ln_kernel(x_ref, g_ref, b_ref, o_ref, ob_ref):
    x = x_ref[...]
    mu = jnp.mean(x, axis=-1, keepdims=True)
    xc = x - mu
    var = jnp.mean(xc * xc, axis=-1, keepdims=True)
    y = xc * jax.lax.rsqrt(var + NORM_EPS) * g_ref[...] + b_ref[...]
    o_ref[...] = y
    ob_ref[...] = y.astype(BF16)


def _layernorm(x, g, b, tm=512):
    T, D = x.shape
    return pl.pallas_call(
        _ln_kernel,
        grid=(T // tm,),
        in_specs=[pl.BlockSpec((tm, D), lambda i: (i, 0)),
                  pl.BlockSpec((1, D), lambda i: (0, 0)),
                  pl.BlockSpec((1, D), lambda i: (0, 0))],
        out_specs=[pl.BlockSpec((tm, D), lambda i: (i, 0)),
                   pl.BlockSpec((tm, D), lambda i: (i, 0))],
        out_shape=[jax.ShapeDtypeStruct((T, D), F32), jax.ShapeDtypeStruct((T, D), BF16)],
        compiler_params=_params("parallel"),
        name="layernorm",
    )(x, g.reshape(1, D), b.reshape(1, D))


def _mm_kernel(x_ref, w_ref, o_ref):
    o_ref[...] = jnp.dot(x_ref[...], w_ref[...], preferred_element_type=F32).astype(o_ref.dtype)


def _matmul(x, w, tm, tn, out_dtype):
    M, K = x.shape
    N = w.shape[1]
    return pl.pallas_call(
        _mm_kernel,
        grid=(M // tm, N // tn),
        in_specs=[pl.BlockSpec((tm, K), lambda i, j: (i, 0)),
                  pl.BlockSpec((K, tn), lambda i, j: (0, j))],
        out_specs=pl.BlockSpec((tm, tn), lambda i, j: (i, j)),
        out_shape=jax.ShapeDtypeStruct((M, N), out_dtype),
        compiler_params=_params("parallel", "parallel"),
        name="in_proj",
    )(x, w)


_SLOT_OF_OCTET = (0, 4, 1, 2, 3, 5, 6, 7)


def _rope(t, tab):
    return t * tab[0] + pltpu.roll(t, LANES // 2, 1) * tab[1]


QBLOCKS = 2


def _attn_kernel(sink_ref, q_ref, ga_ref, kp_ref, kc_ref, kn_ref, vp_ref, vc_ref, vn_ref,
                 tp_ref, tc_ref, tn_ref, bias0_ref, bias1_ref, g_ref, o_ref, acc_ref):
    nkeys = 3 * BLOCK
    bias_refs = (bias0_ref, bias1_ref)
    tab = jnp.concatenate([tp_ref[...], tc_ref[...], tn_ref[...]], axis=1)
    v_all = jnp.concatenate([vp_ref[...], vc_ref[...], vn_ref[...]], axis=0)
    lane = jax.lax.broadcasted_iota(jnp.int32, (1, LANES), 1)
    head_a = (lane % 16) < 8
    scale = HEAD_DIM ** -0.5 * LOG2E

    ks = []
    for c in range(KV_COLS // LANES):
        cols = slice(c * LANES, (c + 1) * LANES)
        kcol = _rope(jnp.concatenate([kp_ref[:, cols], kc_ref[:, cols], kn_ref[:, cols]],
                                     axis=0).astype(F32), tab)
        even_a = jnp.where(head_a, kcol, 0.0)
        odd_b = jnp.where(head_a, 0.0, kcol)
        ks.append((even_a.astype(BF16), pltpu.roll(even_a, 8, 1).astype(BF16)))
        ks.append((pltpu.roll(odd_b, LANES - 8, 1).astype(BF16), odd_b.astype(BF16)))

    for blk in range(QBLOCKS):
        qrows = slice(blk * BLOCK, (blk + 1) * BLOCK)
        krows = slice(blk * BLOCK, blk * BLOCK + nkeys)
        tabq = tc_ref[:, qrows, :]
        bias = bias_refs[blk][...]
        for g in range(N_KV_HEADS):
            kst = jnp.concatenate([ks[g][0][krows], ks[g][1][krows]], axis=0)
            q2 = jnp.concatenate(
                [(_rope(q_ref[qrows, (2 * g + pr) * LANES:(2 * g + pr + 1) * LANES].astype(F32),
                        tabq) * scale).astype(BF16) for pr in range(2)], axis=0)
            st = jax.lax.dot_general(kst, q2, (((1,), (1,)), ((), ())),
                                     preferred_element_type=F32) + bias
            vg = v_all[krows, g * HEAD_DIM:(g + 1) * HEAD_DIM]
            for par in range(2):
                for pr in range(2):
                    h = 4 * g + 2 * pr + par
                    s = st[par * nkeys:(par + 1) * nkeys, pr * BLOCK:(pr + 1) * BLOCK]
                    sk = sink_ref[h] * LOG2E
                    m = jnp.maximum(jnp.max(s, axis=0, keepdims=True), sk)
                    p = jnp.exp2(s - m)
                    denom = jnp.sum(p, axis=0, keepdims=True) + jnp.exp2(sk - m)
                    pn = (p * (1.0 / denom)).astype(BF16)
                    o = jax.lax.dot_general(pn, vg, (((0,), (0,)), ((), ())),
                                            preferred_element_type=F32)
                    acc_ref[qrows, h * HEAD_DIM:(h + 1) * HEAD_DIM] = o

        a = acc_ref[qrows, :]
        an = a * jax.lax.rsqrt(jnp.mean(a * a, axis=-1, keepdims=True) + NORM_EPS) * g_ref[...]
        ga = ga_ref[qrows, :].astype(F32)
        o_ref[qrows, :] = (an * (ga * jax.nn.sigmoid(ga))).astype(o_ref.dtype)


def _attention(proj3, rope_tab, bias, sink, attn_g):
    B, L, _ = proj3.shape
    nb = L // BLOCK
    tq = QBLOCKS * BLOCK
    ns = L // tq
    assert QBLOCKS == 2 and L % tq == 0
    prev = lambda n: jnp.maximum(QBLOCKS * n - 1, 0)
    nxt = lambda n: jnp.minimum(QBLOCKS * (n + 1), nb - 1)

    def kv_specs(col):
        return [pl.BlockSpec((None, BLOCK, KV_COLS), lambda b, n: (b, prev(n), col)),
                pl.BlockSpec((None, tq, KV_COLS), lambda b, n: (b, n, col)),
                pl.BlockSpec((None, BLOCK, KV_COLS), lambda b, n: (b, nxt(n), col))]

    tab_specs = [pl.BlockSpec((2, BLOCK, LANES), lambda b, n: (0, prev(n), 0)),
                 pl.BlockSpec((2, tq, LANES), lambda b, n: (0, n, 0)),
                 pl.BlockSpec((2, BLOCK, LANES), lambda b, n: (0, nxt(n), 0))]
    bias_specs = [pl.BlockSpec((None, 6 * BLOCK, 2 * BLOCK),
                               lambda b, n: (jnp.where(n == 0, 0, 1), 0, 0)),
                  pl.BlockSpec((None, 6 * BLOCK, 2 * BLOCK),
                               lambda b, n: (jnp.where(n == ns - 1, 2, 1), 0, 0))]
    return pl.pallas_call(
        _attn_kernel,
        grid=(B, ns),
        in_specs=[pl.BlockSpec(memory_space=pltpu.SMEM),
                  pl.BlockSpec((None, tq, Q_COLS), lambda b, n: (b, n, COL_Q)),
                  pl.BlockSpec((None, tq, D_ATTN), lambda b, n: (b, n, COL_GA)),
                  *kv_specs(COL_K), *kv_specs(COL_V), *tab_specs, *bias_specs,
                  pl.BlockSpec((1, D_ATTN), lambda b, n: (0, 0))],
        out_specs=pl.BlockSpec((None, tq, D_ATTN), lambda b, n: (b, n, 0)),
        out_shape=jax.ShapeDtypeStruct((B, L, D_ATTN), BF16),
        scratch_shapes=[pltpu.VMEM((tq, D_ATTN), F32)],
        compiler_params=_params("parallel", "parallel"),
        name="window_attention",
    )(sink, proj3, proj3, proj3, proj3, proj3, proj3, proj3, proj3,
      rope_tab, rope_tab, rope_tab, bias, bias, attn_g.reshape(1, D_ATTN))


def _rope_table(L):
    inv = ROPE_THETA ** (-jnp.arange(0, ROT_DIM, 2, dtype=F32) / ROT_DIM)
    lane = np.arange(LANES)
    slot = lane // 16
    rotary = (slot == 0) | (slot == 4)
    inv_lane = jnp.where(jnp.asarray(rotary), inv[lane % 8], 0.0)
    sign = jnp.asarray(np.where(slot == 0, -1.0, np.where(slot == 4, 1.0, 0.0)).astype(np.float32))
    ang = jnp.arange(L, dtype=F32)[:, None] * inv_lane[None, :]
    return jnp.stack([jnp.cos(ang), jnp.sin(ang) * sign[None, :]])


def _attn_bias():
    c = np.arange(3 * BLOCK)[:, None]
    r = np.arange(BLOCK)[None, :]
    band = (c >= r) & (c <= r + 2 * WINDOW)
    variants = [band & (c >= BLOCK), band, band & (c < 2 * BLOCK)]
    out = np.stack([np.tile(np.where(v, 0.0, MASK_VALUE), (2, 2)) for v in variants])
    return jnp.asarray(out.astype(np.float32))


HALO = PACK


def _proj_conv_kernel(xp_ref, x_ref, xn_ref, w_ref, cw_ref, cb_ref, o_ref, res_ref, *, L, chunk):
    tm = x_ref.shape[0]
    i = pl.program_id(0)
    xx = jnp.concatenate([xp_ref[...], x_ref[...], xn_ref[...]], axis=0)
    res_ref[...] = jnp.dot(xx, w_ref[...], preferred_element_type=F32)
    before = res_ref[HALO - SUB:HALO, :]
    res_ref[HALO - SUB:HALO, :] = jnp.where((i * tm) % L == 0, 0.0, before)
    after = res_ref[HALO + tm:HALO + tm + SUB, :]
    res_ref[HALO + tm:HALO + tm + SUB, :] = jnp.where(((i + 1) * tm) % L == 0, 0.0, after)

    w0, w1, w2, b = cw_ref[0:1, :], cw_ref[1:2, :], cw_ref[2:3, :], cb_ref[...]
    for r0 in range(0, tm, chunk):
        xm = res_ref[HALO - 1 + r0:HALO - 1 + r0 + chunk, :]
        x0 = res_ref[HALO + r0:HALO + r0 + chunk, :]
        xq = res_ref[HALO + 1 + r0:HALO + 1 + r0 + chunk, :]
        o_ref[r0:r0 + chunk, :] = (xm * w0 + x0 * w1 + xq * w2 + b).astype(o_ref.dtype)


def _proj_conv(x, w, conv_w, conv_b, L, tm=1024, tn=1024):
    T, K = x.shape
    N = w.shape[1]
    assert L % tm == 0 and tm % HALO == 0
    nh = T // HALO
    per = tm // HALO
    return pl.pallas_call(
        functools.partial(_proj_conv_kernel, L=L, chunk=256),
        grid=(T // tm, N // tn),
        in_specs=[pl.BlockSpec((HALO, K), lambda i, j: (jnp.maximum(i * per - 1, 0), 0)),
                  pl.BlockSpec((tm, K), lambda i, j: (i, 0)),
                  pl.BlockSpec((HALO, K), lambda i, j: (jnp.minimum((i + 1) * per, nh - 1), 0)),
                  pl.BlockSpec((K, tn), lambda i, j: (0, j)),
                  pl.BlockSpec((3, tn), lambda i, j: (0, j)),
                  pl.BlockSpec((1, tn), lambda i, j: (0, j))],
        out_specs=pl.BlockSpec((tm, tn), lambda i, j: (i, j)),
        out_shape=jax.ShapeDtypeStruct((T, N), BF16),
        scratch_shapes=[pltpu.VMEM((tm + 2 * HALO, tn), F32)],
        compiler_params=_params("parallel", "parallel"),
        name="in_proj_conv",
    )(x, x, x, w, conv_w, conv_b.reshape(1, -1))


def _split(a):
    hi = a.astype(BF16)
    return hi, (a - hi.astype(F32)).astype(BF16)


def _dot3(a, b):
    ah, al = _split(a)
    bh, bl = _split(b)
    d = lambda x, y: jnp.dot(x, y, preferred_element_type=F32)
    return d(ah, bh) + (d(ah, bl) + d(al, bh))


def _dot3_tn(at, b):
    ah, al = _split(at)
    bh, bl = _split(b)
    a3 = jnp.concatenate([ah, al, ah], axis=0)
    b3 = jnp.concatenate([bh, bh, bl], axis=0)
    return jax.lax.dot_general(a3, b3, (((0,), (0,)), ((), ())), preferred_element_type=F32)


def _fmlp_kernel(bands_ref, w1t_ref, w1c_ref, w1s_ref, b1_ref, f1_ref, w2_ref, b2_ref, f2_ref,
                 w3f_ref, w3b_ref, df_ref, db_ref, of_ref, ob_ref, nrm_ref, *, L, tr):
    i = pl.program_id(0)
    n = tr + LANES
    lag = (i * tr + jax.lax.broadcasted_iota(jnp.int32, (1, n), 1)).astype(F32)
    t_row = lag / (L - 1)
    ang = bands_ref[...] * (2.0 * math.pi * lag / L)
    pre = (w1t_ref[...] * t_row + _dot3(w1c_ref[...], jnp.cos(ang))
           + _dot3(w1s_ref[...], -jnp.sin(ang)) + b1_ref[...])
    h = jnp.sin(f1_ref[...] * pre)
    h = jnp.sin(f2_ref[...] * (_dot3(w2_ref[...], h) + b2_ref[...]))

    h_a = h[:, :tr]
    h_b = pltpu.roll(h, n - 1, 1)[:, :tr]
    pos = (i * tr + jax.lax.broadcasted_iota(jnp.int32, (tr, 1), 0)).astype(F32)
    t_a = pos / (L - 1)
    t_b = (pos + 1.0) / (L - 1)
    dec_b = jnp.abs(db_ref[...])
    out_f = _dot3_tn(h_a, w3f_ref[...]) * jnp.exp(-t_a * jnp.abs(df_ref[...]))
    out_b = _dot3_tn(h_b, w3b_ref[...]) * jnp.exp(-t_b * dec_b)
    out_b = jnp.where(pos + 1.0 <= L - 1, out_b, 0.0)
    of_ref[...] = out_f.astype(of_ref.dtype)
    ob_ref[...] = out_b.astype(ob_ref.dtype)
    part = (jnp.sum(jnp.abs(out_f), axis=0, keepdims=True)
            + jnp.sum(jnp.abs(out_b), axis=0, keepdims=True))

    @pl.when(i == 0)
    def _():
        b0 = _dot3_tn(h[:, :LANES], w3b_ref[...])
        nrm_ref[...] = part + jnp.abs(b0[0:1])

    @pl.when(i > 0)
    def _():
        nrm_ref[...] += part


def _filter_mlp(L, w1, b1, f1, w2, b2, f2, w3, decay, tr=256):
    nf = 2 * D_HYENA
    w3r = w3.reshape(FILTER_HIDDEN, 2, 2, D_HYENA)
    dr = decay.reshape(2, 2, D_HYENA)
    w3f, w3b = w3r[:, :, 0].reshape(FILTER_HIDDEN, nf), w3r[:, :, 1].reshape(FILTER_HIDDEN, nf)
    df, db = dr[:, 0].reshape(1, nf), dr[:, 1].reshape(1, nf)
    bands = jnp.linspace(1e-4, FILTER_BANDS - 1, FILTER_BANDS, dtype=F32).reshape(FILTER_BANDS, 1)
    H = FILTER_HIDDEN
    col = lambda v: v.reshape(H, 1)
    full = lambda shape: pl.BlockSpec(shape, lambda i: (0,) * len(shape))
    return pl.pallas_call(
        functools.partial(_fmlp_kernel, L=L, tr=tr),
        grid=(L // tr,),
        in_specs=[full((FILTER_BANDS, 1)), full((H, 1)), full((H, FILTER_BANDS)),
                  full((H, FILTER_BANDS)), full((H, 1)), full((H, 1)), full((H, H)),
                  full((H, 1)), full((H, 1)), full((H, nf)), full((H, nf)),
                  full((1, nf)), full((1, nf))],
        out_specs=[pl.BlockSpec((tr, nf), lambda i: (i, 0)),
                   pl.BlockSpec((tr, nf), lambda i: (i, 0)),
                   pl.BlockSpec((1, nf), lambda i: (0, 0))],
        out_shape=[jax.ShapeDtypeStruct((L, nf), BF16), jax.ShapeDtypeStruct((L, nf), BF16),
                   jax.ShapeDtypeStruct((1, nf), F32)],
        compiler_params=_params("arbitrary"),
        name="filter_mlp",
    )(bands, w1[0:1].T, w1[1:1 + FILTER_BANDS].T, w1[1 + FILTER_BANDS:].T, col(b1), col(f1),
      w2.T, col(b2), col(f2), w3f, w3b, df, db)


def _cs(num, den):
    ang = 2.0 * np.pi * (np.asarray(num, np.int64) % den).astype(np.float64) / den
    return np.cos(ang), np.sin(ang)


def _dft_constants(L, pairs):
    n = 2 * L
    n1 = 128 if L >= 8192 else 64
    n2 = n // n1
    n1h = n1 // 2
    q = n1h if pairs else n1 // 4
    ar = np.arange
    eye = np.eye(SUB)
    bf = lambda m: jnp.asarray(m.astype(np.float32)).astype(BF16)

    c, s = _cs(ar(n1)[:, None] * ar(q)[None, :], n1)
    f1 = np.stack([np.concatenate([c, s], axis=1), np.concatenate([-s, c], axis=1)], axis=1)
    k1 = np.kron(f1.reshape(2 * n1, 2 * q), eye)

    c, s = _cs(ar(n1)[:, None] * ar(n1h)[None, :], n1)
    k1f = np.kron(np.stack([c, -s], axis=1).reshape(2 * n1, n1h), eye)
    c, s = _cs(ar(n1)[:, None] * (n1 - 1 - ar(n1h))[None, :], n1)
    k1b = np.kron(np.stack([c, -s], axis=1).reshape(2 * n1, n1h), eye[::-1])

    c2, s2 = (jnp.asarray(x.astype(np.float32)) for x in _cs(ar(n2)[:, None] * ar(n2)[None, :], n2))
    ct, st = (jnp.asarray(x.astype(np.float32)) for x in _cs(ar(n1)[:, None] * ar(n2)[None, :], n))
    gr = c2[None] * ct[:, None, :] - s2[None] * st[:, None, :]
    gi = -(s2[None] * ct[:, None, :] + c2[None] * st[:, None, :])
    gs = jnp.concatenate([jnp.concatenate([gr, -gi], axis=2),
                          jnp.concatenate([gi, gr], axis=2)], axis=1).astype(BF16)

    c, s = _cs(ar(n1h)[:, None] * ar(n1)[None, :], n1)
    if pairs:
        top = np.stack([c, -s], axis=2).reshape(n1h, 2 * n1)
        bot = np.stack([s, c], axis=2).reshape(n1h, 2 * n1)
        f3 = np.concatenate([top, bot], axis=0)
    else:
        c2h, s2h = _cs((ar(n1h)[:, None] - q) * ar(n1)[None, :], n1)
        f3 = np.stack([c + s2h, -s + c2h], axis=2).reshape(n1h, 2 * n1)
    k3 = np.kron(f3 / n, eye)
    return dict(n1=n1, n2=n2, q=q, k1=bf(k1), k1f=bf(k1f), k1b=bf(k1b), gs=gs, k3=bf(k3))


def _halves(x):
    cb = x.shape[-1]
    return [x[..., h * SUB:(h + 1) * SUB, :].reshape(-1, cb).astype(BF16) for h in range(2)]


def _join(lo, hi, lead):
    cb = lo.shape[-1]
    return jnp.concatenate([lo.reshape(*lead, SUB, cb), hi.reshape(*lead, SUB, cb)], axis=len(lead))


INNER_LANES = 256


def _inner_forward(scr_ref, k1_idx, c0, gs):
    n2 = scr_ref.shape[2]
    a = scr_ref[k1_idx, :, :, c0:c0 + INNER_LANES].reshape(2 * n2, INNER_LANES)
    return jnp.dot(gs, a, preferred_element_type=F32)


def _conv_kernel(u_ref, g_ref, d_ref, k1_ref, k3_ref, gs_ref, kf_ref, o_ref, scr_ref,
                 *, s1, s2):
    n1, _, n2, cb = scr_ref.shape
    kb = gs_ref.shape[0]
    q = u_ref.shape[1]
    s = pl.program_id(2)

    @pl.when(s < s1)
    def _():
        for p0 in range(0, GROUP, PACK):
            x = u_ref[:, :, p0:p0 + PACK, :].astype(F32)
            lo, hi = [jnp.dot(k1_ref[...], xh, preferred_element_type=F32) for xh in _halves(x)]
            row0 = pl.multiple_of(s * GROUP + p0, PACK)
            scr_ref[:, :, pl.ds(row0, PACK), :] = _join(lo, hi, (n1, 2)).astype(BF16)

    @pl.when((s >= s1) & (s < s1 + s2))
    def _():
        for kk in range(kb):
            k1_idx = (s - s1) * kb + kk
            for c0 in range(0, cb, INNER_LANES):
                u = _inner_forward(scr_ref, k1_idx, c0, gs_ref[kk])
                ur, ui = u[:n2], u[n2:]
                kr = kf_ref[kk, 0, :, c0:c0 + INNER_LANES].astype(F32)
                ki = kf_ref[kk, 1, :, c0:c0 + INNER_LANES].astype(F32)
                v = jnp.concatenate([ur * kr - ui * ki, ur * ki + ui * kr], axis=0).astype(BF16)
                b = jax.lax.dot_general(gs_ref[kk], v, (((0,), (0,)), ((), ())),
                                        preferred_element_type=F32)
                scr_ref[k1_idx, :, :, c0:c0 + INNER_LANES] = (
                    b.reshape(2, n2, INNER_LANES).astype(BF16))

    @pl.when(s >= s1 + s2)
    def _():
        for p0 in range(0, GROUP, PACK):
            row0 = pl.multiple_of((s - s1 - s2) * GROUP + p0, PACK)
            b = scr_ref[:, :, pl.ds(row0, PACK), :].astype(F32)
            lo, hi = [jnp.dot(k3_ref[...], x, preferred_element_type=F32) for x in _halves(b)]
            y = _join(lo, hi, (2, q))
            rows = slice(p0, p0 + PACK)
            z = g_ref[:, :, rows, :].astype(F32) * (y + u_ref[:, :, rows, :].astype(F32) * d_ref[...])
            o_ref[:, :, rows, :] = z.astype(o_ref.dtype)


def _long_conv(u6, u_off, g6, g_off, d, kf, order, dc, cb, kb):
    bp, _, q, n2p, _, _ = u6.shape
    C = D_HYENA
    n1, n2 = dc["n1"], dc["n2"]
    s1, s2 = n2p, n1 // kb
    koff = order * (C // cb)
    grp1 = lambda s: jnp.where(s < s1, s, jnp.where(s < s1 + s2, s1 - 1, s - s1 - s2))
    grp3 = lambda s: jnp.maximum(s - s1 - s2, 0)
    kblk = lambda s: jnp.clip(s - s1, 0, s2 - 1)
    tspec = lambda off, grp: pl.BlockSpec((None, 2, q, None, GROUP, cb),
                                          lambda b, c, s: (b, 0, 0, grp(s), 0, off + c))
    const = lambda a: pl.BlockSpec(a.shape, lambda b, c, s: (0,) * a.ndim,
                                   pipeline_mode=pl.Buffered(1))
    return pl.pallas_call(
        functools.partial(_conv_kernel, s1=s1, s2=s2),
        grid=(bp, C // cb, s1 + s2 + s1),
        in_specs=[tspec(u_off, grp1), tspec(g_off, grp3),
                  pl.BlockSpec((1, cb), lambda b, c, s: (0, c)),
                  const(dc["k1"]), const(dc["k3"]),
                  pl.BlockSpec((kb, 2 * n2, 2 * n2), lambda b, c, s: (kblk(s), 0, 0)),
                  pl.BlockSpec((kb, 2, n2, cb), lambda b, c, s: (kblk(s), 0, 0, koff + c))],
        out_specs=tspec(0, grp3),
        out_shape=jax.ShapeDtypeStruct((bp, 2, q, n2p, GROUP, C), BF16),
        scratch_shapes=[pltpu.VMEM((n1, 2, n2, cb), BF16)],
        compiler_params=_params("parallel", "parallel", "arbitrary"),
        name="long_conv",
    )(u6, g6, d.reshape(1, C), dc["k1"], dc["k3"], dc["gs"], kf)


def _hyena(hyc, kf, dc, hy_d, cb, kb):
    B, L, _ = hyc.shape
    C = D_HYENA
    n2, q = dc["n2"], dc["q"]
    bp = B * L // (2 * q * n2)
    hy6 = hyc.reshape(bp, 2, q, n2 // GROUP, GROUP, 3 * C)
    z6 = _long_conv(hy6, 0, hy6, C // cb, hy_d[0], kf, 0, dc, cb, kb)
    z6 = _long_conv(z6, 0, hy6, 2 * (C // cb), hy_d[1], kf, 1, dc, cb, kb)
    return z6.reshape(B, L, C)


def _filt_kernel(xf_ref, xb_ref, nrm_ref, k1f_ref, k1b_ref, gs_ref, o_ref, scr_ref, *, s1):
    n1, _, n2, cb = scr_ref.shape
    kb = gs_ref.shape[0]
    s = pl.program_id(1)

    @pl.when(s < s1)
    def _():
        d = lambda k, x: jnp.dot(k[...], x, preferred_element_type=F32)
        for p0 in range(0, GROUP, PACK):
            f_lo, f_hi = _halves(xf_ref[:, p0:p0 + PACK, :].astype(F32))
            b_lo, b_hi = _halves(xb_ref[:, GROUP - PACK - p0:GROUP - p0, :].astype(F32))
            lo = d(k1f_ref, f_lo) + d(k1b_ref, b_hi)
            hi = d(k1f_ref, f_hi) + d(k1b_ref, b_lo)
            row0 = pl.multiple_of(s * GROUP + p0, PACK)
            scr_ref[:, :, pl.ds(row0, PACK), :] = _join(lo, hi, (n1, 2)).astype(BF16)

    @pl.when(s >= s1)
    def _():
        for kk in range(kb):
            for c0 in range(0, cb, INNER_LANES):
                inv = 1.0 / nrm_ref[:, c0:c0 + INNER_LANES]
                u = _inner_forward(scr_ref, (s - s1) * kb + kk, c0, gs_ref[kk]) * inv
                o_ref[kk, :, :, c0:c0 + INNER_LANES] = (
                    u.reshape(2, n2, INNER_LANES).astype(o_ref.dtype))


def _filter_spectrum(L, dc, fw1, fb1, ff1, fw2, fb2, ff2, fw3, fdecay, cb, kb):
    hf, hb, nrm = _filter_mlp(L, fw1, fb1, ff1, fw2, fb2, ff2, fw3, fdecay)
    nf = hf.shape[1]
    n1, n2 = dc["n1"], dc["n2"]
    n1h, n2p = n1 // 2, n2 // GROUP
    s1, s2 = n2p, n1 // kb
    hf4 = hf.reshape(n1h, n2p, GROUP, nf)
    hb4 = hb.reshape(n1h, n2p, GROUP, nf)
    grp = lambda s: jnp.minimum(s, s1 - 1)
    kblk = lambda s: jnp.maximum(s - s1, 0)
    const = lambda a: pl.BlockSpec(a.shape, lambda c, s: (0,) * a.ndim,
                                   pipeline_mode=pl.Buffered(1))
    return pl.pallas_call(
        functools.partial(_filt_kernel, s1=s1),
        grid=(nf // cb, s1 + s2),
        in_specs=[pl.BlockSpec((n1h, None, GROUP, cb), lambda c, s: (0, grp(s), 0, c)),
                  pl.BlockSpec((n1h, None, GROUP, cb), lambda c, s: (0, s1 - 1 - grp(s), 0, c)),
                  pl.BlockSpec((1, cb), lambda c, s: (0, c)),
                  const(dc["k1f"]), const(dc["k1b"]),
                  pl.BlockSpec((kb, 2 * n2, 2 * n2), lambda c, s: (kblk(s), 0, 0))],
        out_specs=pl.BlockSpec((kb, 2, n2, cb), lambda c, s: (kblk(s), 0, 0, c)),
        out_shape=jax.ShapeDtypeStruct((n1, 2, n2, nf), BF16),
        scratch_shapes=[pltpu.VMEM((n1, 2, n2, cb), BF16)],
        compiler_params=_params("parallel", "arbitrary"),
        name="filter_spectrum",
    )(hf4, hb4, nrm, dc["k1f"], dc["k1b"], dc["gs"])


OUT_CHAINS = 2


def _out_kernel(a_ref, z_ref, gh_ref, hg_ref, wa_ref, wz_ref, h_ref, g_ref, b_ref, o_ref,
                *maybe_ob_ref):
    chain = a_ref.shape[0] // OUT_CHAINS
    for r0 in range(0, a_ref.shape[0], chain):
        rows = slice(r0, r0 + chain)
        z = z_ref[rows, :].astype(F32)
        gh = gh_ref[rows, :].astype(F32)
        zn = z * jax.lax.rsqrt(jnp.mean(z * z, axis=-1, keepdims=True) + NORM_EPS) * hg_ref[...]
        zn = (zn * (gh * jax.nn.sigmoid(gh))).astype(BF16)
        acc = (jnp.dot(a_ref[rows, :], wa_ref[...], preferred_element_type=F32)
               + jnp.dot(zn, wz_ref[...], preferred_element_type=F32))
        y = DN_ALPHA * h_ref[rows, :] + acc
        mu = jnp.mean(y, axis=-1, keepdims=True)
        yc = y - mu
        var = jnp.mean(yc * yc, axis=-1, keepdims=True)
        out = yc * jax.lax.rsqrt(var + NORM_EPS) * g_ref[...] + b_ref[...]
        o_ref[rows, :] = out
        for ob_ref in maybe_ob_ref:
            ob_ref[rows, :] = out.astype(BF16)


def _out_proj(a, z, proj, hy_g, w_out, h, ln_g, ln_b, want_bf16, tm=512):
    T, D = h.shape
    n_out = 2 if want_bf16 else 1
    wa = w_out[:D_ATTN].astype(BF16)
    wz = w_out[D_ATTN:].astype(BF16)
    row = lambda w: pl.BlockSpec((tm, w), lambda i: (i, 0))
    const = lambda r, w: pl.BlockSpec((r, w), lambda i: (0, 0), pipeline_mode=pl.Buffered(1))
    return pl.pallas_call(
        _out_kernel,
        grid=(T // tm,),
        in_specs=[row(D_ATTN), row(D_HYENA),
                  pl.BlockSpec((tm, D_HYENA), lambda i: (i, COL_GH)),
                  const(1, D_HYENA), const(D_ATTN, D), const(D_HYENA, D), row(D),
                  const(1, D), const(1, D)],
        out_specs=[row(D), row(D)][:n_out],
        out_shape=[jax.ShapeDtypeStruct((T, D), F32), jax.ShapeDtypeStruct((T, D), BF16)][:n_out],
        compiler_params=_params("parallel"),
        name="out_proj",
    )(a, z, proj, hy_g.reshape(1, -1), wa, wz, h, ln_g.reshape(1, D), ln_b.reshape(1, D))


def _pair_lane_order(w):
    rows = w.shape[0]
    w5 = w.reshape(rows, -1, 2, 8, 8)
    octet_of_slot = np.argsort(_SLOT_OF_OCTET)
    w5 = jnp.concatenate([w5[:, :, :, o:o + 1, :] for o in octet_of_slot], axis=3)
    return w5.transpose(0, 1, 3, 2, 4).reshape(rows, -1)


def _arrange_w_in(w):
    q, k, v, ga, hy, gh = jnp.split(w, [1024, 1280, 1536, 2560, 5632], axis=1)
    plain = jnp.concatenate([_pair_lane_order(q), ga, gh, _pair_lane_order(k), v], axis=1)
    return plain.astype(BF16), hy.astype(BF16)


def _trunk(x, p):
    B, L, D = x.shape
    T = B * L
    pairs = B > 1
    dc = _dft_constants(L, pairs)
    cb, kb, kb_filt = (512, 32, 32) if L <= 2048 else (256, 16, 16)
    rope_tab = _rope_table(L)
    bias = _attn_bias()
    h, hb = _layernorm(x.reshape(T, D), p["emb_ln_g"], p["emb_ln_b"])
    for l in range(DEPTH):
        w_plain, w_hy = _arrange_w_in(p["w_in"][l])
        proj = _matmul(hb, w_plain, 1024, D_PLAIN // 2, BF16)
        proj3 = proj.reshape(B, L, D_PLAIN)
        a = _attention(proj3, rope_tab, bias, p["attn_sink"][l], p["attn_norm_g"][l])
        hyc = _proj_conv(hb, w_hy, p["conv_w"][l], p["conv_b"][l], L).reshape(B, L, 3 * D_HYENA)
        kf = _filter_spectrum(L, dc, p["flt_w1"][l], p["flt_b1"][l], p["flt_freq1"][l],
                              p["flt_w2"][l], p["flt_b2"][l], p["flt_freq2"][l],
                              p["flt_w3"][l], p["flt_decay"][l], cb, kb_filt)
        z = _hyena(hyc, kf, dc, p["hyena_d"][l], cb, kb)
        outs = _out_proj(a.reshape(T, D_ATTN), z.reshape(T, D_HYENA), proj,
                         p["hyena_norm_g"][l], p["w_out"][l], h, p["ln_g"][l], p["ln_b"][l],
                         want_bf16=l + 1 < DEPTH)
        h, hb = outs[0], outs[-1]
    return h.reshape(B, L, D)


def kernel(x_prompt, x_sample, emb_ln_g, emb_ln_b, w_in, attn_sink, conv_w, conv_b, flt_w1,
           flt_b1, flt_freq1, flt_w2, flt_b2, flt_freq2, flt_w3, flt_decay, hyena_d,
           attn_norm_g, hyena_norm_g, w_out, ln_g, ln_b):
    p = dict(emb_ln_g=emb_ln_g, emb_ln_b=emb_ln_b, w_in=w_in, attn_sink=attn_sink,
             conv_w=conv_w, conv_b=conv_b, flt_w1=flt_w1, flt_b1=flt_b1, flt_freq1=flt_freq1,
             flt_w2=flt_w2, flt_b2=flt_b2, flt_freq2=flt_freq2, flt_w3=flt_w3,
             flt_decay=flt_decay, hyena_d=hyena_d, attn_norm_g=attn_norm_g,
             hyena_norm_g=hyena_norm_g, w_out=w_out, ln_g=ln_g, ln_b=ln_b)
    return (_trunk(x_prompt, p), _trunk(x_sample, p))
```

```python
import functools
import math

import numpy as np
import jax
import jax.numpy as jnp
from jax.experimental import pallas as pl
from jax.experimental.pallas import tpu as pltpu

F32 = jnp.float32
BF16 = jnp.bfloat16

D_MODEL = 2048
DEPTH = 2
D_ATTN = 1024
D_HYENA = 1024
HEAD_DIM = 64
N_HEADS = 16
N_KV_HEADS = 4
GQA_GROUPS = 4
ROT_DIM = 16
ROPE_THETA = 500000.0
WINDOW = 128
BLOCK = 128
FILTER_BANDS = 16
FILTER_HIDDEN = 64
DN_ALPHA = (2.0 * DEPTH) ** 0.25
NORM_EPS = 1e-5
MASK_VALUE = -1e30
LOG2E = math.log2(math.e)
Q_COLS = N_HEADS * HEAD_DIM
KV_COLS = N_KV_HEADS * HEAD_DIM
D_IN = 2 * Q_COLS + 2 * KV_COLS + 4 * D_HYENA

COL_Q, COL_GA, COL_GH = 0, 1, 2
COL_K, COL_V = 12, 13
D_PLAIN = 3 * 1024 + 2 * KV_COLS

LANES = 128
SUB = 8
PACK = 16
GROUP = 2 * PACK
V7X_VMEM_BYTES = 64 * 1024 * 1024
VMEM_LIMIT = V7X_VMEM_BYTES * 3 // 4


def _params(*sem):
    return pltpu.CompilerParams(dimension_semantics=sem, vmem_limit_bytes=VMEM_LIMIT)


def _ln_kernel(x_ref, g_ref, b_ref, o_ref, ob_ref):
    x = x_ref[...]
    mu = jnp.mean(x, axis=-1, keepdims=True)
    xc = x - mu
    var = jnp.mean(xc * xc, axis=-1, keepdims=True)
    y = xc * jax.lax.rsqrt(var + NORM_EPS) * g_ref[...] + b_ref[...]
    o_ref[...] = y
    ob_ref[...] = y.astype(BF16)


def _layernorm(x, g, b, tm=512):
    T, D = x.shape
    return pl.pallas_call(
        _ln_kernel,
        grid=(T // tm,),
        in_specs=[pl.BlockSpec((tm, D), lambda i: (i, 0)),
                  pl.BlockSpec((1, D), lambda i: (0, 0)),
                  pl.BlockSpec((1, D), lambda i: (0, 0))],
        out_specs=[pl.BlockSpec((tm, D), lambda i: (i, 0)),
                   pl.BlockSpec((tm, D), lambda i: (i, 0))],
        out_shape=[jax.ShapeDtypeStruct((T, D), F32), jax.ShapeDtypeStruct((T, D), BF16)],
        compiler_params=_params("parallel"),
        name="layernorm",
    )(x, g.reshape(1, D), b.reshape(1, D))


def _mm_kernel(x_ref, w_ref, o_ref):
    o_ref[...] = jnp.dot(x_ref[...], w_ref[...], preferred_element_type=F32).astype(o_ref.dtype)


def _matmul(x, w, tm, tn, out_dtype):
    M, K = x.shape
    N = w.shape[1]
    return pl.pallas_call(
        _mm_kernel,
        grid=(M // tm, N // tn),
        in_specs=[pl.BlockSpec((tm, K), lambda i, j: (i, 0)),
                  pl.BlockSpec((K, tn), lambda i, j: (0, j))],
        out_specs=pl.BlockSpec((tm, tn), lambda i, j: (i, j)),
        out_shape=jax.ShapeDtypeStruct((M, N), out_dtype),
        compiler_params=_params("parallel", "parallel"),
        name="in_proj",
    )(x, w)


_SLOT_OF_OCTET = (0, 4, 1, 2, 3, 5, 6, 7)


def _rope(t, tab):
    return t * tab[0] + pltpu.roll(t, LANES // 2, 1) * tab[1]


QBLOCKS = 4


def _attn_kernel(sink_ref, q_ref, ga_ref, kp_ref, kc_ref, kn_ref, vp_ref, vc_ref, vn_ref,
                 tp_ref, tc_ref, tn_ref, bias_first_ref, bias_mid_ref, bias_last_ref, g_ref, o_ref,
                 acc_ref):
    nkeys = 3 * BLOCK
    bias_refs = (bias_first_ref,) + (bias_mid_ref,) * (QBLOCKS - 2) + (bias_last_ref,)
    tab = jnp.concatenate([tp_ref[...], tc_ref[...], tn_ref[...]], axis=1)
    v_all = jnp.concatenate([vp_ref[...], vc_ref[...], vn_ref[...]], axis=0)
    lane = jax.lax.broadcasted_iota(jnp.int32, (1, LANES), 1)
    head_a = (lane % 16) < 8
    scale = HEAD_DIM ** -0.5 * LOG2E

    ks = []
    for c in range(KV_COLS // LANES):
        cols = slice(c * LANES, (c + 1) * LANES)
        kcol = _rope(jnp.concatenate([kp_ref[:, cols], kc_ref[:, cols], kn_ref[:, cols]],
                                     axis=0).astype(F32), tab)
        even_a = jnp.where(head_a, kcol, 0.0)
        odd_b = jnp.where(head_a, 0.0, kcol)
        ks.append((even_a.astype(BF16), pltpu.roll(even_a, 8, 1).astype(BF16)))
        ks.append((pltpu.roll(odd_b, LANES - 8, 1).astype(BF16), odd_b.astype(BF16)))

    for blk in range(QBLOCKS):
        qrows = slice(blk * BLOCK, (blk + 1) * BLOCK)
        krows = slice(blk * BLOCK, blk * BLOCK + nkeys)
        tabq = tc_ref[:, qrows, :]
        bias = bias_refs[blk][...]
        for g in range(N_KV_HEADS):
            kst = jnp.concatenate([ks[g][0][krows], ks[g][1][krows]], axis=0)
            q2 = jnp.concatenate(
                [(_rope(q_ref[qrows, (2 * g + pr) * LANES:(2 * g + pr + 1) * LANES].astype(F32),
                        tabq) * scale).astype(BF16) for pr in range(2)], axis=0)
            st = jax.lax.dot_general(kst, q2, (((1,), (1,)), ((), ())),
                                     preferred_element_type=F32) + bias
            vg = v_all[krows, g * HEAD_DIM:(g + 1) * HEAD_DIM]
            for par in range(2):
                for pr in range(2):
                    h = 4 * g + 2 * pr + par
                    s = st[par * nkeys:(par + 1) * nkeys, pr * BLOCK:(pr + 1) * BLOCK]
                    sk = sink_ref[h] * LOG2E
                    m = jnp.maximum(jnp.max(s, axis=0, keepdims=True), sk)
                    p = jnp.exp2(s - m)
                    denom = jnp.sum(p, axis=0, keepdims=True) + jnp.exp2(sk - m)
                    pn = (p * (1.0 / denom)).astype(BF16)
                    o = jax.lax.dot_general(pn, vg, (((0,), (0,)), ((), ())),
                                            preferred_element_type=F32)
                    acc_ref[qrows, h * HEAD_DIM:(h + 1) * HEAD_DIM] = o

        a = acc_ref[qrows, :]
        an = a * jax.lax.rsqrt(jnp.mean(a * a, axis=-1, keepdims=True) + NORM_EPS) * g_ref[...]
        ga = ga_ref[qrows, :].astype(F32)
        o_ref[qrows, :] = (an * (ga * jax.nn.sigmoid(ga))).astype(o_ref.dtype)


def _attention(proj3, rope_tab, bias, sink, attn_g):
    B, L, _ = proj3.shape
    nb = L // BLOCK
    tq = QBLOCKS * BLOCK
    ns = L // tq
    assert QBLOCKS >= 2 and L % tq == 0
    prev = lambda n: jnp.maximum(QBLOCKS * n - 1, 0)
    nxt = lambda n: jnp.minimum(QBLOCKS * (n + 1), nb - 1)

    def kv_specs(col):
        return [pl.BlockSpec((None, BLOCK, KV_COLS), lambda b, n: (b, prev(n), col)),
                pl.BlockSpec((None, tq, KV_COLS), lambda b, n: (b, n, col)),
                pl.BlockSpec((None, BLOCK, KV_COLS), lambda b, n: (b, nxt(n), col))]

    tab_specs = [pl.BlockSpec((2, BLOCK, LANES), lambda b, n: (0, prev(n), 0)),
                 pl.BlockSpec((2, tq, LANES), lambda b, n: (0, n, 0)),
                 pl.BlockSpec((2, BLOCK, LANES), lambda b, n: (0, nxt(n), 0))]
    bias_specs = [pl.BlockSpec((None, 6 * BLOCK, 2 * BLOCK),
                               lambda b, n: (jnp.where(n == 0, 0, 1), 0, 0)),
                  pl.BlockSpec((None, 6 * BLOCK, 2 * BLOCK), lambda b, n: (1, 0, 0)),
                  pl.BlockSpec((None, 6 * BLOCK, 2 * BLOCK),
                               lambda b, n: (jnp.where(n == ns - 1, 2, 1), 0, 0))]
    return pl.pallas_call(
        _attn_kernel,
        grid=(B, ns),
        in_specs=[pl.BlockSpec(memory_space=pltpu.SMEM),
                  pl.BlockSpec((None, tq, Q_COLS), lambda b, n: (b, n, COL_Q)),
                  pl.BlockSpec((None, tq, D_ATTN), lambda b, n: (b, n, COL_GA)),
                  *kv_specs(COL_K), *kv_specs(COL_V), *tab_specs, *bias_specs,
                  pl.BlockSpec((1, D_ATTN), lambda b, n: (0, 0))],
        out_specs=pl.BlockSpec((None, tq, D_ATTN), lambda b, n: (b, n, 0)),
        out_shape=jax.ShapeDtypeStruct((B, L, D_ATTN), BF16),
        scratch_shapes=[pltpu.VMEM((tq, D_ATTN), F32)],
        compiler_params=_params("parallel", "parallel"),
        name="window_attention",
    )(sink, proj3, proj3, proj3, proj3, proj3, proj3, proj3, proj3,
      rope_tab, rope_tab, rope_tab, bias, bias, bias, attn_g.reshape(1, D_ATTN))


def _rope_table(L):
    inv = ROPE_THETA ** (-jnp.arange(0, ROT_DIM, 2, dtype=F32) / ROT_DIM)
    lane = np.arange(LANES)
    slot = lane // 16
    rotary = (slot == 0) | (slot == 4)
    inv_lane = jnp.where(jnp.asarray(rotary), inv[lane % 8], 0.0)
    sign = jnp.asarray(np.where(slot == 0, -1.0, np.where(slot == 4, 1.0, 0.0)).astype(np.float32))
    ang = jnp.arange(L, dtype=F32)[:, None] * inv_lane[None, :]
    return jnp.stack([jnp.cos(ang), jnp.sin(ang) * sign[None, :]])


def _attn_bias():
    c = np.arange(3 * BLOCK)[:, None]
    r = np.arange(BLOCK)[None, :]
    band = (c >= r) & (c <= r + 2 * WINDOW)
    variants = [band & (c >= BLOCK), band, band & (c < 2 * BLOCK)]
    out = np.stack([np.tile(np.where(v, 0.0, MASK_VALUE), (2, 2)) for v in variants])
    return jnp.asarray(out.astype(np.float32))


HALO = PACK


def _proj_conv_kernel(xp_ref, x_ref, xn_ref, w_ref, cw_ref, cb_ref, o_ref, res_ref, *, L, chunk):
    tm = x_ref.shape[0]
    i = pl.program_id(0)
    xx = jnp.concatenate([xp_ref[...], x_ref[...], xn_ref[...]], axis=0)
    res_ref[...] = jnp.dot(xx, w_ref[...], preferred_element_type=F32)
    before = res_ref[HALO - SUB:HALO, :]
    res_ref[HALO - SUB:HALO, :] = jnp.where((i * tm) % L == 0, 0.0, before)
    after = res_ref[HALO + tm:HALO + tm + SUB, :]
    res_ref[HALO + tm:HALO + tm + SUB, :] = jnp.where(((i + 1) * tm) % L == 0, 0.0, after)

    w0, w1, w2, b = cw_ref[0:1, :], cw_ref[1:2, :], cw_ref[2:3, :], cb_ref[...]
    for r0 in range(0, tm, chunk):
        win = res_ref[HALO - SUB + r0:HALO + SUB + r0 + chunk, :]
        xm = pltpu.roll(win, 1, 0)[SUB:SUB + chunk]
        xq = pltpu.roll(win, chunk + 2 * SUB - 1, 0)[SUB:SUB + chunk]
        x0 = win[SUB:SUB + chunk]
        o_ref[r0:r0 + chunk, :] = (xm * w0 + x0 * w1 + xq * w2 + b).astype(o_ref.dtype)


def _proj_conv(x, w, conv_w, conv_b, L, tm=1024, tn=1024):
    T, K = x.shape
    N = w.shape[1]
    assert L % tm == 0 and tm % HALO == 0
    nh = T // HALO
    per = tm // HALO
    return pl.pallas_call(
        functools.partial(_proj_conv_kernel, L=L, chunk=256),
        grid=(T // tm, N // tn),
        in_specs=[pl.BlockSpec((HALO, K), lambda i, j: (jnp.maximum(i * per - 1, 0), 0)),
                  pl.BlockSpec((tm, K), lambda i, j: (i, 0)),
                  pl.BlockSpec((HALO, K), lambda i, j: (jnp.minimum((i + 1) * per, nh - 1), 0)),
                  pl.BlockSpec((K, tn), lambda i, j: (0, j)),
                  pl.BlockSpec((3, tn), lambda i, j: (0, j)),
                  pl.BlockSpec((1, tn), lambda i, j: (0, j))],
        out_specs=pl.BlockSpec((tm, tn), lambda i, j: (i, j)),
        out_shape=jax.ShapeDtypeStruct((T, N), BF16),
        scratch_shapes=[pltpu.VMEM((tm + 2 * HALO, tn), F32)],
        compiler_params=_params("parallel", "parallel"),
        name="in_proj_conv",
    )(x, x, x, w, conv_w, conv_b.reshape(1, -1))


def _split(a):
    hi = a.astype(BF16)
    return hi, (a - hi.astype(F32)).astype(BF16)


def _dot3(a, b):
    ah, al = _split(a)
    bh, bl = _split(b)
    d = lambda x, y: jnp.dot(x, y, preferred_element_type=F32)
    return d(ah, bh) + (d(ah, bl) + d(al, bh))


def _dot3_tn(at, b):
    ah, al = _split(at)
    bh, bl = _split(b)
    a3 = jnp.concatenate([ah, al, ah], axis=0)
    b3 = jnp.concatenate([bh, bh, bl], axis=0)
    return jax.lax.dot_general(a3, b3, (((0,), (0,)), ((), ())), preferred_element_type=F32)


def _fmlp_kernel(bands_ref, w1t_ref, w1c_ref, w1s_ref, b1_ref, f1_ref, w2_ref, b2_ref, f2_ref,
                 w3f_ref, w3b_ref, df_ref, db_ref, of_ref, ob_ref, nrm_ref, *, L, tr):
    i = pl.program_id(0)
    n = tr + LANES
    lag = (i * tr + jax.lax.broadcasted_iota(jnp.int32, (1, n), 1)).astype(F32)
    t_row = lag / (L - 1)
    ang = bands_ref[...] * (2.0 * math.pi * lag / L)
    pre = (w1t_ref[...] * t_row + _dot3(w1c_ref[...], jnp.cos(ang))
           + _dot3(w1s_ref[...], -jnp.sin(ang)) + b1_ref[...])
    h = jnp.sin(f1_ref[...] * pre)
    h = jnp.sin(f2_ref[...] * (_dot3(w2_ref[...], h) + b2_ref[...]))

    h_a = h[:, :tr]
    h_b = pltpu.roll(h, n - 1, 1)[:, :tr]
    pos = (i * tr + jax.lax.broadcasted_iota(jnp.int32, (tr, 1), 0)).astype(F32)
    t_a = pos / (L - 1)
    t_b = (pos + 1.0) / (L - 1)
    dec_b = jnp.abs(db_ref[...])
    out_f = _dot3_tn(h_a, w3f_ref[...]) * jnp.exp(-t_a * jnp.abs(df_ref[...]))
    out_b = _dot3_tn(h_b, w3b_ref[...]) * jnp.exp(-t_b * dec_b)
    out_b = jnp.where(pos + 1.0 <= L - 1, out_b, 0.0)
    of_ref[...] = out_f.astype(of_ref.dtype)
    ob_ref[...] = out_b.astype(ob_ref.dtype)
    part = (jnp.sum(jnp.abs(out_f), axis=0, keepdims=True)
            + jnp.sum(jnp.abs(out_b), axis=0, keepdims=True))

    @pl.when(i == 0)
    def _():
        b0 = _dot3_tn(h[:, :LANES], w3b_ref[...])
        nrm_ref[...] = part + jnp.abs(b0[0:1])

    @pl.when(i > 0)
    def _():
        nrm_ref[...] += part


def _filter_mlp(L, w1, b1, f1, w2, b2, f2, w3, decay, tr=256):
    nf = 2 * D_HYENA
    w3r = w3.reshape(FILTER_HIDDEN, 2, 2, D_HYENA)
    dr = decay.reshape(2, 2, D_HYENA)
    w3f, w3b = w3r[:, :, 0].reshape(FILTER_HIDDEN, nf), w3r[:, :, 1].reshape(FILTER_HIDDEN, nf)
    df, db = dr[:, 0].reshape(1, nf), dr[:, 1].reshape(1, nf)
    bands = jnp.linspace(1e-4, FILTER_BANDS - 1, FILTER_BANDS, dtype=F32).reshape(FILTER_BANDS, 1)
    H = FILTER_HIDDEN
    col = lambda v: v.reshape(H, 1)
    full = lambda shape: pl.BlockSpec(shape, lambda i: (0,) * len(shape))
    return pl.pallas_call(
        functools.partial(_fmlp_kernel, L=L, tr=tr),
        grid=(L // tr,),
        in_specs=[full((FILTER_BANDS, 1)), full((H, 1)), full((H, FILTER_BANDS)),
                  full((H, FILTER_BANDS)), full((H, 1)), full((H, 1)), full((H, H)),
                  full((H, 1)), full((H, 1)), full((H, nf)), full((H, nf)),
                  full((1, nf)), full((1, nf))],
        out_specs=[pl.BlockSpec((tr, nf), lambda i: (i, 0)),
                   pl.BlockSpec((tr, nf), lambda i: (i, 0)),
                   pl.BlockSpec((1, nf), lambda i: (0, 0))],
        out_shape=[jax.ShapeDtypeStruct((L, nf), BF16), jax.ShapeDtypeStruct((L, nf), BF16),
                   jax.ShapeDtypeStruct((1, nf), F32)],
        compiler_params=_params("arbitrary"),
        name="filter_mlp",
    )(bands, w1[0:1].T, w1[1:1 + FILTER_BANDS].T, w1[1 + FILTER_BANDS:].T, col(b1), col(f1),
      w2.T, col(b2), col(f2), w3f, w3b, df, db)


def _cs(num, den):
    ang = 2.0 * np.pi * (np.asarray(num, np.int64) % den).astype(np.float64) / den
    return np.cos(ang), np.sin(ang)


def _dft_constants(L, pairs):
    n = 2 * L
    n1 = 128 if L >= 8192 else 64
    n2 = n // n1
    n1h = n1 // 2
    q = n1h if pairs else n1 // 4
    ar = np.arange
    eye = np.eye(SUB)
    bf = lambda m: jnp.asarray(m.astype(np.float32)).astype(BF16)

    c, s = _cs(ar(n1)[:, None] * ar(q)[None, :], n1)
    f1 = np.stack([np.concatenate([c, s], axis=1), np.concatenate([-s, c], axis=1)], axis=1)
    k1 = np.kron(f1.reshape(2 * n1, 2 * q), eye)

    c, s = _cs(ar(n1)[:, None] * ar(n1h)[None, :], n1)
    k1f = np.kron(np.stack([c, -s], axis=1).reshape(2 * n1, n1h), eye)
    c, s = _cs(ar(n1)[:, None] * (n1 - 1 - ar(n1h))[None, :], n1)
    k1b = np.kron(np.stack([c, -s], axis=1).reshape(2 * n1, n1h), eye[::-1])

    c2, s2 = (jnp.asarray(x.astype(np.float32)) for x in _cs(ar(n2)[:, None] * ar(n2)[None, :], n2))
    ct, st = (jnp.asarray(x.astype(np.float32)) for x in _cs(ar(n1)[:, None] * ar(n2)[None, :], n))
    gr = c2[None] * ct[:, None, :] - s2[None] * st[:, None, :]
    gi = -(s2[None] * ct[:, None, :] + c2[None] * st[:, None, :])
    gs = jnp.concatenate([jnp.concatenate([gr, -gi], axis=2),
                          jnp.concatenate([gi, gr], axis=2)], axis=1).astype(BF16)

    c, s = _cs(ar(n1h)[:, None] * ar(n1)[None, :], n1)
    if pairs:
        top = np.stack([c, -s], axis=2).reshape(n1h, 2 * n1)
        bot = np.stack([s, c], axis=2).reshape(n1h, 2 * n1)
        f3 = np.concatenate([top, bot], axis=0)
    else:
        c2h, s2h = _cs((ar(n1h)[:, None] - q) * ar(n1)[None, :], n1)
        f3 = np.stack([c + s2h, -s + c2h], axis=2).reshape(n1h, 2 * n1)
    k3 = np.kron(f3 / n, eye)
    return dict(n1=n1, n2=n2, q=q, k1=bf(k1), k1f=bf(k1f), k1b=bf(k1b), gs=gs, k3=bf(k3))


def _halves(x):
    cb = x.shape[-1]
    return [x[..., h * SUB:(h + 1) * SUB, :].reshape(-1, cb).astype(BF16) for h in range(2)]


def _join(lo, hi, lead):
    cb = lo.shape[-1]
    return jnp.concatenate([lo.reshape(*lead, SUB, cb), hi.reshape(*lead, SUB, cb)], axis=len(lead))


INNER_LANES = 256


def _inner_forward(scr_ref, k1_idx, c0, gs):
    n2 = scr_ref.shape[2]
    a = scr_ref[k1_idx, :, :, c0:c0 + INNER_LANES].reshape(2 * n2, INNER_LANES)
    return jnp.dot(gs, a, preferred_element_type=F32)


def _conv_kernel(u_ref, g_ref, d_ref, k1_ref, k3_ref, gs_ref, kf_ref, o_ref, scr_ref,
                 *, s1, s2):
    n1, _, n2, cb = scr_ref.shape
    kb = gs_ref.shape[0]
    q = u_ref.shape[1]
    s = pl.program_id(2)

    @pl.when(s < s1)
    def _():
        for p0 in range(0, GROUP, PACK):
            x = u_ref[:, :, p0:p0 + PACK, :].astype(F32)
            lo, hi = [jnp.dot(k1_ref[...], xh, preferred_element_type=F32) for xh in _halves(x)]
            row0 = pl.multiple_of(s * GROUP + p0, PACK)
            scr_ref[:, :, pl.ds(row0, PACK), :] = _join(lo, hi, (n1, 2)).astype(BF16)

    @pl.when((s >= s1) & (s < s1 + s2))
    def _():
        for kk in range(kb):
            k1_idx = (s - s1) * kb + kk
            for c0 in range(0, cb, INNER_LANES):
                u = _inner_forward(scr_ref, k1_idx, c0, gs_ref[kk])
                ur, ui = u[:n2], u[n2:]
                kr = kf_ref[kk, 0, :, c0:c0 + INNER_LANES].astype(F32)
                ki = kf_ref[kk, 1, :, c0:c0 + INNER_LANES].astype(F32)
                v = jnp.concatenate([ur * kr - ui * ki, ur * ki + ui * kr], axis=0).astype(BF16)
                b = jax.lax.dot_general(gs_ref[kk], v, (((0,), (0,)), ((), ())),
                                        preferred_element_type=F32)
                scr_ref[k1_idx, :, :, c0:c0 + INNER_LANES] = (
                    b.reshape(2, n2, INNER_LANES).astype(BF16))

    @pl.when(s >= s1 + s2)
    def _():
        for p0 in range(0, GROUP, PACK):
            row0 = pl.multiple_of((s - s1 - s2) * GROUP + p0, PACK)
            b = scr_ref[:, :, pl.ds(row0, PACK), :].astype(F32)
            lo, hi = [jnp.dot(k3_ref[...], x, preferred_element_type=F32) for x in _halves(b)]
            y = _join(lo, hi, (2, q))
            rows = slice(p0, p0 + PACK)
            z = g_ref[:, :, rows, :].astype(F32) * (y + u_ref[:, :, rows, :].astype(F32) * d_ref[...])
            o_ref[:, :, rows, :] = z.astype(o_ref.dtype)


def _long_conv(u6, u_off, g6, g_off, d, kf, order, dc, cb, kb):
    bp, _, q, n2p, _, _ = u6.shape
    C = D_HYENA
    n1, n2 = dc["n1"], dc["n2"]
    s1, s2 = n2p, n1 // kb
    koff = order * (C // cb)
    grp1 = lambda s: jnp.where(s < s1, s, jnp.where(s < s1 + s2, s1 - 1, s - s1 - s2))
    grp3 = lambda s: jnp.maximum(s - s1 - s2, 0)
    kblk = lambda s: jnp.clip(s - s1, 0, s2 - 1)
    tspec = lambda off, grp: pl.BlockSpec((None, 2, q, None, GROUP, cb),
                                          lambda b, c, s: (b, 0, 0, grp(s), 0, off + c))
    const = lambda a: pl.BlockSpec(a.shape, lambda b, c, s: (0,) * a.ndim,
                                   pipeline_mode=pl.Buffered(1))
    return pl.pallas_call(
        functools.partial(_conv_kernel, s1=s1, s2=s2),
        grid=(bp, C // cb, s1 + s2 + s1),
        in_specs=[tspec(u_off, grp1), tspec(g_off, grp3),
                  pl.BlockSpec((1, cb), lambda b, c, s: (0, c)),
                  const(dc["k1"]), const(dc["k3"]),
                  pl.BlockSpec((kb, 2 * n2, 2 * n2), lambda b, c, s: (kblk(s), 0, 0)),
                  pl.BlockSpec((kb, 2, n2, cb), lambda b, c, s: (kblk(s), 0, 0, koff + c))],
        out_specs=tspec(0, grp3),
        out_shape=jax.ShapeDtypeStruct((bp, 2, q, n2p, GROUP, C), BF16),
        scratch_shapes=[pltpu.VMEM((n1, 2, n2, cb), BF16)],
        compiler_params=_params("parallel", "parallel", "arbitrary"),
        name="long_conv",
    )(u6, g6, d.reshape(1, C), dc["k1"], dc["k3"], dc["gs"], kf)


def _hyena(hyc, kf, dc, hy_d, cb, kb):
    B, L, _ = hyc.shape
    C = D_HYENA
    n2, q = dc["n2"], dc["q"]
    bp = B * L // (2 * q * n2)
    hy6 = hyc.reshape(bp, 2, q, n2 // GROUP, GROUP, 3 * C)
    z6 = _long_conv(hy6, 0, hy6, C // cb, hy_d[0], kf, 0, dc, cb, kb)
    z6 = _long_conv(z6, 0, hy6, 2 * (C // cb), hy_d[1], kf, 1, dc, cb, kb)
    return z6.reshape(B, L, C)


def _filt_kernel(xf_ref, xb_ref, nrm_ref, k1f_ref, k1b_ref, gs_ref, o_ref, scr_ref, *, s1):
    n1, _, n2, cb = scr_ref.shape
    kb = gs_ref.shape[0]
    s = pl.program_id(1)

    @pl.when(s < s1)
    def _():
        d = lambda k, x: jnp.dot(k[...], x, preferred_element_type=F32)
        for p0 in range(0, GROUP, PACK):
            f_lo, f_hi = _halves(xf_ref[:, p0:p0 + PACK, :].astype(F32))
            b_lo, b_hi = _halves(xb_ref[:, GROUP - PACK - p0:GROUP - p0, :].astype(F32))
            lo = d(k1f_ref, f_lo) + d(k1b_ref, b_hi)
            hi = d(k1f_ref, f_hi) + d(k1b_ref, b_lo)
            row0 = pl.multiple_of(s * GROUP + p0, PACK)
            scr_ref[:, :, pl.ds(row0, PACK), :] = _join(lo, hi, (n1, 2)).astype(BF16)

    @pl.when(s >= s1)
    def _():
        for kk in range(kb):
            for c0 in range(0, cb, INNER_LANES):
                inv = 1.0 / nrm_ref[:, c0:c0 + INNER_LANES]
                u = _inner_forward(scr_ref, (s - s1) * kb + kk, c0, gs_ref[kk]) * inv
                o_ref[kk, :, :, c0:c0 + INNER_LANES] = (
                    u.reshape(2, n2, INNER_LANES).astype(o_ref.dtype))


def _filter_spectrum(L, dc, fw1, fb1, ff1, fw2, fb2, ff2, fw3, fdecay, cb, kb):
    hf, hb, nrm = _filter_mlp(L, fw1, fb1, ff1, fw2, fb2, ff2, fw3, fdecay)
    nf = hf.shape[1]
    n1, n2 = dc["n1"], dc["n2"]
    n1h, n2p = n1 // 2, n2 // GROUP
    s1, s2 = n2p, n1 // kb
    hf4 = hf.reshape(n1h, n2p, GROUP, nf)
    hb4 = hb.reshape(n1h, n2p, GROUP, nf)
    grp = lambda s: jnp.minimum(s, s1 - 1)
    kblk = lambda s: jnp.maximum(s - s1, 0)
    const = lambda a: pl.BlockSpec(a.shape, lambda c, s: (0,) * a.ndim,
                                   pipeline_mode=pl.Buffered(1))
    return pl.pallas_call(
        functools.partial(_filt_kernel, s1=s1),
        grid=(nf // cb, s1 + s2),
        in_specs=[pl.BlockSpec((n1h, None, GROUP, cb), lambda c, s: (0, grp(s), 0, c)),
                  pl.BlockSpec((n1h, None, GROUP, cb), lambda c, s: (0, s1 - 1 - grp(s), 0, c)),
                  pl.BlockSpec((1, cb), lambda c, s: (0, c)),
                  const(dc["k1f"]), const(dc["k1b"]),
                  pl.BlockSpec((kb, 2 * n2, 2 * n2), lambda c, s: (kblk(s), 0, 0))],
        out_specs=pl.BlockSpec((kb, 2, n2, cb), lambda c, s: (kblk(s), 0, 0, c)),
        out_shape=jax.ShapeDtypeStruct((n1, 2, n2, nf), BF16),
        scratch_shapes=[pltpu.VMEM((n1, 2, n2, cb), BF16)],
        compiler_params=_params("parallel", "arbitrary"),
        name="filter_spectrum",
    )(hf4, hb4, nrm, dc["k1f"], dc["k1b"], dc["gs"])


OUT_CHAINS = 2


def _out_kernel(a_ref, z_ref, gh_ref, hg_ref, w_ref, h_ref, g_ref, b_ref, o_ref,
                *maybe_ob_ref):
    chain = a_ref.shape[0] // OUT_CHAINS
    for r0 in range(0, a_ref.shape[0], chain):
        rows = slice(r0, r0 + chain)
        z = z_ref[rows, :].astype(F32)
        gh = gh_ref[rows, :].astype(F32)
        zn = z * jax.lax.rsqrt(jnp.mean(z * z, axis=-1, keepdims=True) + NORM_EPS) * hg_ref[...]
        zn = (zn * (gh * jax.nn.sigmoid(gh))).astype(BF16)
        acc = jnp.dot(jnp.concatenate([a_ref[rows, :], zn], axis=1), w_ref[...],
                      preferred_element_type=F32)
        y = DN_ALPHA * h_ref[rows, :] + acc
        mu = jnp.mean(y, axis=-1, keepdims=True)
        yc = y - mu
        var = jnp.mean(yc * yc, axis=-1, keepdims=True)
        out = yc * jax.lax.rsqrt(var + NORM_EPS) * g_ref[...] + b_ref[...]
        o_ref[rows, :] = out
        for ob_ref in maybe_ob_ref:
            ob_ref[rows, :] = out.astype(BF16)


def _out_proj(a, z, proj, hy_g, w_out, h, ln_g, ln_b, want_bf16, tm=512):
    T, D = h.shape
    n_out = 2 if want_bf16 else 1
    row = lambda w: pl.BlockSpec((tm, w), lambda i: (i, 0))
    const = lambda r, w: pl.BlockSpec((r, w), lambda i: (0, 0), pipeline_mode=pl.Buffered(1))
    return pl.pallas_call(
        _out_kernel,
        grid=(T // tm,),
        in_specs=[row(D_ATTN), row(D_HYENA),
                  pl.BlockSpec((tm, D_HYENA), lambda i: (i, COL_GH)),
                  const(1, D_HYENA), const(D_ATTN + D_HYENA, D), row(D),
                  const(1, D), const(1, D)],
        out_specs=[row(D), row(D)][:n_out],
        out_shape=[jax.ShapeDtypeStruct((T, D), F32), jax.ShapeDtypeStruct((T, D), BF16)][:n_out],
        compiler_params=_params("parallel"),
        name="out_proj",
    )(a, z, proj, hy_g.reshape(1, -1), w_out.astype(BF16), h, ln_g.reshape(1, D),
      ln_b.reshape(1, D))


def _pair_lane_order(w):
    rows = w.shape[0]
    w5 = w.reshape(rows, -1, 2, 8, 8)
    octet_of_slot = np.argsort(_SLOT_OF_OCTET)
    w5 = jnp.concatenate([w5[:, :, :, o:o + 1, :] for o in octet_of_slot], axis=3)
    return w5.transpose(0, 1, 3, 2, 4).reshape(rows, -1)


def _arrange_w_in(w):
    q, k, v, ga, hy, gh = jnp.split(w, [1024, 1280, 1536, 2560, 5632], axis=1)
    plain = jnp.concatenate([_pair_lane_order(q), ga, gh, _pair_lane_order(k), v], axis=1)
    return plain.astype(BF16), hy.astype(BF16)


def _trunk(x, p):
    B, L, D = x.shape
    T = B * L
    pairs = B > 1
    dc = _dft_constants(L, pairs)
    cb, kb, kb_filt = (512, 32, 32) if L <= 2048 else (256, 16, 16)
    rope_tab = _rope_table(L)
    bias = _attn_bias()
    h, hb = _layernorm(x.reshape(T, D), p["emb_ln_g"], p["emb_ln_b"])
    for l in range(DEPTH):
        w_plain, w_hy = _arrange_w_in(p["w_in"][l])
        proj = _matmul(hb, w_plain, 1024, D_PLAIN // 2, BF16)
        proj3 = proj.reshape(B, L, D_PLAIN)
        a = _attention(proj3, rope_tab, bias, p["attn_sink"][l], p["attn_norm_g"][l])
        hyc = _proj_conv(hb, w_hy, p["conv_w"][l], p["conv_b"][l], L).reshape(B, L, 3 * D_HYENA)
        kf = _filter_spectrum(L, dc, p["flt_w1"][l], p["flt_b1"][l], p["flt_freq1"][l],
                              p["flt_w2"][l], p["flt_b2"][l], p["flt_freq2"][l],
                              p["flt_w3"][l], p["flt_decay"][l], cb, kb_filt)
        z = _hyena(hyc, kf, dc, p["hyena_d"][l], cb, kb)
        outs = _out_proj(a.reshape(T, D_ATTN), z.reshape(T, D_HYENA), proj,
                         p["hyena_norm_g"][l], p["w_out"][l], h, p["ln_g"][l], p["ln_b"][l],
                         want_bf16=l + 1 < DEPTH)
        h, hb = outs[0], outs[-1]
    return h.reshape(B, L, D)


def kernel(x_prompt, x_sample, emb_ln_g, emb_ln_b, w_in, attn_sink, conv_w, conv_b, flt_w1,
           flt_b1, flt_freq1, flt_w2, flt_b2, flt_freq2, flt_w3, flt_decay, hyena_d,
           attn_norm_g, hyena_norm_g, w_out, ln_g, ln_b):
    p = dict(emb_ln_g=emb_ln_g, emb_ln_b=emb_ln_b, w_in=w_in, attn_sink=attn_sink,
             conv_w=conv_w, conv_b=conv_b, flt_w1=flt_w1, flt_b1=flt_b1, flt_freq1=flt_freq1,
             flt_w2=flt_w2, flt_b2=flt_b2, flt_freq2=flt_freq2, flt_w3=flt_w3,
             flt_decay=flt_decay, hyena_d=hyena_d, attn_norm_g=attn_norm_g,
             hyena_norm_g=hyena_norm_g, w_out=w_out, ln_g=ln_g, ln_b=ln_b)
    return (_trunk(x_prompt, p), _trunk(x_sample, p))
```

```python
import functools
import math

import numpy as np
import jax
import jax.numpy as jnp
from jax.experimental import pallas as pl
from jax.experimental.pallas import tpu as pltpu

F32 = jnp.float32
BF16 = jnp.bfloat16

DEPTH = 2
D_ATTN = 1024
D_HYENA = 1024
HEAD_DIM = 64
N_HEADS = 16
N_KV_HEADS = 4
ROT_DIM = 16
ROPE_THETA = 500000.0
WINDOW = 128
BLOCK = 128
FILTER_BANDS = 16
FILTER_HIDDEN = 64
DN_ALPHA = (2.0 * DEPTH) ** 0.25
NORM_EPS = 1e-5
MASK_VALUE = -1e30
LOG2E = math.log2(math.e)
Q_COLS = N_HEADS * HEAD_DIM
KV_COLS = N_KV_HEADS * HEAD_DIM

COL_Q, COL_GA, COL_GH = 0, 1, 2
COL_K, COL_V = 12, 13
D_PLAIN = 3 * 1024 + 2 * KV_COLS

LANES = 128
SUB = 8
PACK = 16
GROUP = 2 * PACK
V7X_VMEM_BYTES = 64 * 1024 * 1024
VMEM_LIMIT = V7X_VMEM_BYTES * 3 // 4


def _params(*sem):
    return pltpu.CompilerParams(dimension_semantics=sem, vmem_limit_bytes=VMEM_LIMIT)


def _ln_kernel(x_ref, g_ref, b_ref, o_ref, ob_ref):
    x = x_ref[...]
    mu = jnp.mean(x, axis=-1, keepdims=True)
    xc = x - mu
    var = jnp.mean(xc * xc, axis=-1, keepdims=True)
    y = xc * jax.lax.rsqrt(var + NORM_EPS) * g_ref[...] + b_ref[...]
    o_ref[...] = y
    ob_ref[...] = y.astype(BF16)


def _layernorm(x, g, b, tm=512):
    T, D = x.shape
    return pl.pallas_call(
        _ln_kernel,
        grid=(T // tm,),
        in_specs=[pl.BlockSpec((tm, D), lambda i: (i, 0)),
                  pl.BlockSpec((1, D), lambda i: (0, 0)),
                  pl.BlockSpec((1, D), lambda i: (0, 0))],
        out_specs=[pl.BlockSpec((tm, D), lambda i: (i, 0)),
                   pl.BlockSpec((tm, D), lambda i: (i, 0))],
        out_shape=[jax.ShapeDtypeStruct((T, D), F32), jax.ShapeDtypeStruct((T, D), BF16)],
        compiler_params=_params("parallel"),
        name="layernorm",
    )(x, g.reshape(1, D), b.reshape(1, D))


def _mm_kernel(x_ref, w_ref, o_ref):
    o_ref[...] = jnp.dot(x_ref[...], w_ref[...], preferred_element_type=F32).astype(o_ref.dtype)


def _matmul(x, w, tm, tn, out_dtype):
    M, K = x.shape
    N = w.shape[1]
    return pl.pallas_call(
        _mm_kernel,
        grid=(M // tm, N // tn),
        in_specs=[pl.BlockSpec((tm, K), lambda i, j: (i, 0)),
                  pl.BlockSpec((K, tn), lambda i, j: (0, j))],
        out_specs=pl.BlockSpec((tm, tn), lambda i, j: (i, j)),
        out_shape=jax.ShapeDtypeStruct((M, N), out_dtype),
        compiler_params=_params("parallel", "parallel"),
        name="in_proj",
    )(x, w)


_SLOT_OF_OCTET = (0, 4, 1, 2, 3, 5, 6, 7)


def _rope(t, tab):
    return t * tab[0] + pltpu.roll(t, LANES // 2, 1) * tab[1]


QBLOCKS = 4


def _attn_kernel(sink_ref, q_ref, ga_ref, kp_ref, kc_ref, kn_ref, vp_ref, vc_ref, vn_ref,
                 tp_ref, tc_ref, tn_ref, bias_first_ref, bias_mid_ref, bias_last_ref, g_ref, o_ref,
                 acc_ref):
    nkeys = 3 * BLOCK
    bias_refs = (bias_first_ref,) + (bias_mid_ref,) * (QBLOCKS - 2) + (bias_last_ref,)
    tab = jnp.concatenate([tp_ref[...], tc_ref[...], tn_ref[...]], axis=1)
    v_all = jnp.concatenate([vp_ref[...], vc_ref[...], vn_ref[...]], axis=0)
    lane = jax.lax.broadcasted_iota(jnp.int32, (1, LANES), 1)
    head_a = (lane % 16) < 8
    scale = HEAD_DIM ** -0.5 * LOG2E

    ks = []
    for c in range(KV_COLS // LANES):
        cols = slice(c * LANES, (c + 1) * LANES)
        kcol = _rope(jnp.concatenate([kp_ref[:, cols], kc_ref[:, cols], kn_ref[:, cols]],
                                     axis=0).astype(F32), tab)
        even_a = jnp.where(head_a, kcol, 0.0)
        odd_b = jnp.where(head_a, 0.0, kcol)
        ks.append((even_a.astype(BF16), pltpu.roll(even_a, 8, 1).astype(BF16)))
        ks.append((pltpu.roll(odd_b, LANES - 8, 1).astype(BF16), odd_b.astype(BF16)))

    for blk in range(QBLOCKS):
        qrows = slice(blk * BLOCK, (blk + 1) * BLOCK)
        krows = slice(blk * BLOCK, blk * BLOCK + nkeys)
        tabq = tc_ref[:, qrows, :]
        bias = bias_refs[blk][...]
        for g in range(N_KV_HEADS):
            kst = jnp.concatenate([ks[g][0][krows], ks[g][1][krows]], axis=0)
            q2 = jnp.concatenate(
                [(_rope(q_ref[qrows, (2 * g + pr) * LANES:(2 * g + pr + 1) * LANES].astype(F32),
                        tabq) * scale).astype(BF16) for pr in range(2)], axis=0)
            st = jax.lax.dot_general(kst, q2, (((1,), (1,)), ((), ())),
                                     preferred_element_type=F32) + bias
            vg = v_all[krows, g * HEAD_DIM:(g + 1) * HEAD_DIM]
            for par in range(2):
                for pr in range(2):
                    h = 4 * g + 2 * pr + par
                    s = st[par * nkeys:(par + 1) * nkeys, pr * BLOCK:(pr + 1) * BLOCK]
                    sk = sink_ref[h] * LOG2E
                    m = jnp.maximum(jnp.max(s, axis=0, keepdims=True), sk)
                    p = jnp.exp2(s - m)
                    denom = jnp.sum(p, axis=0, keepdims=True) + jnp.exp2(sk - m)
                    pn = (p * (1.0 / denom)).astype(BF16)
                    o = jax.lax.dot_general(pn, vg, (((0,), (0,)), ((), ())),
                                            preferred_element_type=F32)
                    acc_ref[qrows, h * HEAD_DIM:(h + 1) * HEAD_DIM] = o

        a = acc_ref[qrows, :]
        an = a * jax.lax.rsqrt(jnp.mean(a * a, axis=-1, keepdims=True) + NORM_EPS) * g_ref[...]
        ga = ga_ref[qrows, :].astype(F32)
        o_ref[qrows, :] = (an * (ga * jax.nn.sigmoid(ga))).astype(o_ref.dtype)


def _attention(proj3, rope_tab, bias, sink, attn_g):
    B, L, _ = proj3.shape
    nb = L // BLOCK
    tq = QBLOCKS * BLOCK
    ns = L // tq
    assert QBLOCKS >= 2 and L % tq == 0
    prev = lambda n: jnp.maximum(QBLOCKS * n - 1, 0)
    nxt = lambda n: jnp.minimum(QBLOCKS * (n + 1), nb - 1)

    def kv_specs(col):
        return [pl.BlockSpec((None, BLOCK, KV_COLS), lambda b, n: (b, prev(n), col)),
                pl.BlockSpec((None, tq, KV_COLS), lambda b, n: (b, n, col)),
                pl.BlockSpec((None, BLOCK, KV_COLS), lambda b, n: (b, nxt(n), col))]

    tab_specs = [pl.BlockSpec((2, BLOCK, LANES), lambda b, n: (0, prev(n), 0)),
                 pl.BlockSpec((2, tq, LANES), lambda b, n: (0, n, 0)),
                 pl.BlockSpec((2, BLOCK, LANES), lambda b, n: (0, nxt(n), 0))]
    bias_specs = [pl.BlockSpec((None, 6 * BLOCK, 2 * BLOCK),
                               lambda b, n: (jnp.where(n == 0, 0, 1), 0, 0)),
                  pl.BlockSpec((None, 6 * BLOCK, 2 * BLOCK), lambda b, n: (1, 0, 0)),
                  pl.BlockSpec((None, 6 * BLOCK, 2 * BLOCK),
                               lambda b, n: (jnp.where(n == ns - 1, 2, 1), 0, 0))]
    return pl.pallas_call(
        _attn_kernel,
        grid=(B, ns),
        in_specs=[pl.BlockSpec(memory_space=pltpu.SMEM),
                  pl.BlockSpec((None, tq, Q_COLS), lambda b, n: (b, n, COL_Q)),
                  pl.BlockSpec((None, tq, D_ATTN), lambda b, n: (b, n, COL_GA)),
                  *kv_specs(COL_K), *kv_specs(COL_V), *tab_specs, *bias_specs,
                  pl.BlockSpec((1, D_ATTN), lambda b, n: (0, 0))],
        out_specs=pl.BlockSpec((None, tq, D_ATTN), lambda b, n: (b, n, 0)),
        out_shape=jax.ShapeDtypeStruct((B, L, D_ATTN), BF16),
        scratch_shapes=[pltpu.VMEM((tq, D_ATTN), F32)],
        compiler_params=_params("parallel", "parallel"),
        name="window_attention",
    )(sink, proj3, proj3, proj3, proj3, proj3, proj3, proj3, proj3,
      rope_tab, rope_tab, rope_tab, bias, bias, bias, attn_g.reshape(1, D_ATTN))


def _rope_table(L):
    inv = ROPE_THETA ** (-jnp.arange(0, ROT_DIM, 2, dtype=F32) / ROT_DIM)
    lane = np.arange(LANES)
    slot = lane // 16
    rotary = (slot == 0) | (slot == 4)
    inv_lane = jnp.where(jnp.asarray(rotary), inv[lane % 8], 0.0)
    sign = jnp.asarray(np.where(slot == 0, -1.0, np.where(slot == 4, 1.0, 0.0)).astype(np.float32))
    ang = jnp.arange(L, dtype=F32)[:, None] * inv_lane[None, :]
    return jnp.stack([jnp.cos(ang), jnp.sin(ang) * sign[None, :]])


def _attn_bias():
    c = np.arange(3 * BLOCK)[:, None]
    r = np.arange(BLOCK)[None, :]
    band = (c >= r) & (c <= r + 2 * WINDOW)
    variants = [band & (c >= BLOCK), band, band & (c < 2 * BLOCK)]
    out = np.stack([np.tile(np.where(v, 0.0, MASK_VALUE), (2, 2)) for v in variants])
    return jnp.asarray(out.astype(np.float32))


HALO = PACK


def _proj_conv_kernel(xp_ref, x_ref, xn_ref, w_ref, cw_ref, cb_ref, o_ref, res_ref, *, L, chunk):
    tm = x_ref.shape[0]
    i = pl.program_id(0)
    xx = jnp.concatenate([xp_ref[...], x_ref[...], xn_ref[...]], axis=0)
    res_ref[...] = jnp.dot(xx, w_ref[...], preferred_element_type=F32)
    before = res_ref[HALO - SUB:HALO, :]
    res_ref[HALO - SUB:HALO, :] = jnp.where((i * tm) % L == 0, 0.0, before)
    after = res_ref[HALO + tm:HALO + tm + SUB, :]
    res_ref[HALO + tm:HALO + tm + SUB, :] = jnp.where(((i + 1) * tm) % L == 0, 0.0, after)

    w0, w1, w2, b = cw_ref[0:1, :], cw_ref[1:2, :], cw_ref[2:3, :], cb_ref[...]
    for r0 in range(0, tm, chunk):
        win = res_ref[HALO - SUB + r0:HALO + SUB + r0 + chunk, :]
        xm = pltpu.roll(win, 1, 0)[SUB:SUB + chunk]
        xq = pltpu.roll(win, chunk + 2 * SUB - 1, 0)[SUB:SUB + chunk]
        x0 = win[SUB:SUB + chunk]
        o_ref[r0:r0 + chunk, :] = (xm * w0 + x0 * w1 + xq * w2 + b).astype(o_ref.dtype)


def _proj_conv(x, w, conv_w, conv_b, L, tm=1024, tn=1536):
    T, K = x.shape
    N = w.shape[1]
    assert L % tm == 0 and tm % HALO == 0 and N % tn == 0
    nh = T // HALO
    per = tm // HALO
    return pl.pallas_call(
        functools.partial(_proj_conv_kernel, L=L, chunk=256),
        grid=(T // tm, N // tn),
        in_specs=[pl.BlockSpec((HALO, K), lambda i, j: (jnp.maximum(i * per - 1, 0), 0)),
                  pl.BlockSpec((tm, K), lambda i, j: (i, 0)),
                  pl.BlockSpec((HALO, K), lambda i, j: (jnp.minimum((i + 1) * per, nh - 1), 0)),
                  pl.BlockSpec((K, tn), lambda i, j: (0, j)),
                  pl.BlockSpec((3, tn), lambda i, j: (0, j)),
                  pl.BlockSpec((1, tn), lambda i, j: (0, j))],
        out_specs=pl.BlockSpec((tm, tn), lambda i, j: (i, j)),
        out_shape=jax.ShapeDtypeStruct((T, N), BF16),
        scratch_shapes=[pltpu.VMEM((tm + 2 * HALO, tn), F32)],
        compiler_params=_params("parallel", "parallel"),
        name="in_proj_conv",
    )(x, x, x, w, conv_w, conv_b.reshape(1, -1))


def _split(a):
    hi = a.astype(BF16)
    return hi, (a - hi.astype(F32)).astype(BF16)


def _dot3(a, b):
    ah, al = _split(a)
    bh, bl = _split(b)
    d = lambda x, y: jnp.dot(x, y, preferred_element_type=F32)
    return d(ah, bh) + (d(ah, bl) + d(al, bh))


def _dot3_tn(at, b):
    ah, al = _split(at)
    bh, bl = _split(b)
    a3 = jnp.concatenate([ah, al, ah], axis=0)
    b3 = jnp.concatenate([bh, bh, bl], axis=0)
    return jax.lax.dot_general(a3, b3, (((0,), (0,)), ((), ())), preferred_element_type=F32)


def _fmlp_kernel(bands_ref, w1t_ref, w1c_ref, w1s_ref, b1_ref, f1_ref, w2_ref, b2_ref, f2_ref,
                 w3f_ref, w3b_ref, df_ref, db_ref, of_ref, ob_ref, nrm_ref, *, L, tr):
    i = pl.program_id(0)
    n = tr + LANES
    lag = (i * tr + jax.lax.broadcasted_iota(jnp.int32, (1, n), 1)).astype(F32)
    t_row = lag / (L - 1)
    ang = bands_ref[...] * (2.0 * math.pi * lag / L)
    pre = (w1t_ref[...] * t_row + _dot3(w1c_ref[...], jnp.cos(ang))
           + _dot3(w1s_ref[...], -jnp.sin(ang)) + b1_ref[...])
    h = jnp.sin(f1_ref[...] * pre)
    h = jnp.sin(f2_ref[...] * (_dot3(w2_ref[...], h) + b2_ref[...]))

    h_a = h[:, :tr]
    h_b = pltpu.roll(h, n - 1, 1)[:, :tr]
    pos = (i * tr + jax.lax.broadcasted_iota(jnp.int32, (tr, 1), 0)).astype(F32)
    t_a = pos / (L - 1)
    t_b = (pos + 1.0) / (L - 1)
    dec_b = jnp.abs(db_ref[...])
    out_f = _dot3_tn(h_a, w3f_ref[...]) * jnp.exp(-t_a * jnp.abs(df_ref[...]))
    out_b = _dot3_tn(h_b, w3b_ref[...]) * jnp.exp(-t_b * dec_b)
    out_b = jnp.where(pos + 1.0 <= L - 1, out_b, 0.0)
    of_ref[...] = out_f.astype(of_ref.dtype)
    ob_ref[...] = out_b.astype(ob_ref.dtype)
    part = (jnp.sum(jnp.abs(out_f), axis=0, keepdims=True)
            + jnp.sum(jnp.abs(out_b), axis=0, keepdims=True))

    @pl.when(i == 0)
    def _():
        b0 = _dot3_tn(h[:, :LANES], w3b_ref[...])
        nrm_ref[...] = part + jnp.abs(b0[0:1])

    @pl.when(i > 0)
    def _():
        nrm_ref[...] += part


def _filter_mlp(L, w1, b1, f1, w2, b2, f2, w3, decay, tr=512):
    nf = 2 * D_HYENA
    w3r = w3.reshape(FILTER_HIDDEN, 2, 2, D_HYENA)
    dr = decay.reshape(2, 2, D_HYENA)
    w3f, w3b = w3r[:, :, 0].reshape(FILTER_HIDDEN, nf), w3r[:, :, 1].reshape(FILTER_HIDDEN, nf)
    df, db = dr[:, 0].reshape(1, nf), dr[:, 1].reshape(1, nf)
    bands = jnp.linspace(1e-4, FILTER_BANDS - 1, FILTER_BANDS, dtype=F32).reshape(FILTER_BANDS, 1)
    H = FILTER_HIDDEN
    col = lambda v: v.reshape(H, 1)
    full = lambda shape: pl.BlockSpec(shape, lambda i: (0,) * len(shape))
    return pl.pallas_call(
        functools.partial(_fmlp_kernel, L=L, tr=tr),
        grid=(L // tr,),
        in_specs=[full((FILTER_BANDS, 1)), full((H, 1)), full((H, FILTER_BANDS)),
                  full((H, FILTER_BANDS)), full((H, 1)), full((H, 1)), full((H, H)),
                  full((H, 1)), full((H, 1)), full((H, nf)), full((H, nf)),
                  full((1, nf)), full((1, nf))],
        out_specs=[pl.BlockSpec((tr, nf), lambda i: (i, 0)),
                   pl.BlockSpec((tr, nf), lambda i: (i, 0)),
                   pl.BlockSpec((1, nf), lambda i: (0, 0))],
        out_shape=[jax.ShapeDtypeStruct((L, nf), BF16), jax.ShapeDtypeStruct((L, nf), BF16),
                   jax.ShapeDtypeStruct((1, nf), F32)],
        compiler_params=_params("arbitrary"),
        name="filter_mlp",
    )(bands, w1[0:1].T, w1[1:1 + FILTER_BANDS].T, w1[1 + FILTER_BANDS:].T, col(b1), col(f1),
      w2.T, col(b2), col(f2), w3f, w3b, df, db)


def _cs(num, den):
    ang = 2.0 * np.pi * (np.asarray(num, np.int64) % den).astype(np.float64) / den
    return np.cos(ang), np.sin(ang)


def _dft_constants(L, pairs):
    n = 2 * L
    n1 = 128 if L >= 8192 else 64
    n2 = n // n1
    n1h = n1 // 2
    q = n1h if pairs else n1 // 4
    ar = np.arange
    eye = np.eye(SUB)
    bf = lambda m: jnp.asarray(m.astype(np.float32)).astype(BF16)

    c, s = _cs(ar(n1)[:, None] * ar(q)[None, :], n1)
    f1 = np.stack([np.concatenate([c, s], axis=1), np.concatenate([-s, c], axis=1)], axis=1)
    k1 = np.kron(f1.reshape(2 * n1, 2 * q), eye)

    c, s = _cs(ar(n1)[:, None] * ar(n1h)[None, :], n1)
    k1f = np.kron(np.stack([c, -s], axis=1).reshape(2 * n1, n1h), eye)
    c, s = _cs(ar(n1)[:, None] * (n1 - 1 - ar(n1h))[None, :], n1)
    k1b = np.kron(np.stack([c, -s], axis=1).reshape(2 * n1, n1h), eye[::-1])

    c2, s2 = (jnp.asarray(x.astype(np.float32)) for x in _cs(ar(n2)[:, None] * ar(n2)[None, :], n2))
    ct, st = (jnp.asarray(x.astype(np.float32)) for x in _cs(ar(n1)[:, None] * ar(n2)[None, :], n))
    gr = c2[None] * ct[:, None, :] - s2[None] * st[:, None, :]
    gi = -(s2[None] * ct[:, None, :] + c2[None] * st[:, None, :])
    gs = jnp.concatenate([jnp.concatenate([gr, -gi], axis=2),
                          jnp.concatenate([gi, gr], axis=2)], axis=1).astype(BF16)

    c, s = _cs(ar(n1h)[:, None] * ar(n1)[None, :], n1)
    if pairs:
        top = np.stack([c, -s], axis=2).reshape(n1h, 2 * n1)
        bot = np.stack([s, c], axis=2).reshape(n1h, 2 * n1)
        f3 = np.concatenate([top, bot], axis=0)
    else:
        c2h, s2h = _cs((ar(n1h)[:, None] - q) * ar(n1)[None, :], n1)
        f3 = np.stack([c + s2h, -s + c2h], axis=2).reshape(n1h, 2 * n1)
    k3 = np.kron(f3 / n, eye)
    return dict(n1=n1, n2=n2, q=q, k1=bf(k1), k1f=bf(k1f), k1b=bf(k1b), gs=gs, k3=bf(k3))


def _halves(x):
    cb = x.shape[-1]
    return [x[..., h * SUB:(h + 1) * SUB, :].reshape(-1, cb).astype(BF16) for h in range(2)]


def _join(lo, hi, lead):
    cb = lo.shape[-1]
    return jnp.concatenate([lo.reshape(*lead, SUB, cb), hi.reshape(*lead, SUB, cb)], axis=len(lead))


GS_RESIDENT_BYTES = 4 * 1024 * 1024
INNER_LANES = 256


def _inner_forward(scr_ref, k1_idx, c0, gs):
    n2 = scr_ref.shape[2]
    a = scr_ref[k1_idx, :, :, c0:c0 + INNER_LANES].reshape(2 * n2, INNER_LANES)
    return jnp.dot(gs, a, preferred_element_type=F32)


def _conv_kernel(u_ref, g_ref, d_ref, k1_ref, k3_ref, gs_ref, kf_ref, o_ref, scr_ref,
                 *, s1, s2):
    n1, _, n2, cb = scr_ref.shape
    kb = kf_ref.shape[0]
    gs_resident = gs_ref.shape[0] == n1
    q = u_ref.shape[1]
    s = pl.program_id(2)

    @pl.when(s < s1)
    def _():
        for p0 in range(0, GROUP, PACK):
            x = u_ref[:, :, p0:p0 + PACK, :].astype(F32)
            lo, hi = [jnp.dot(k1_ref[...], xh, preferred_element_type=F32) for xh in _halves(x)]
            row0 = pl.multiple_of(s * GROUP + p0, PACK)
            scr_ref[:, :, pl.ds(row0, PACK), :] = _join(lo, hi, (n1, 2)).astype(BF16)

    @pl.when((s >= s1) & (s < s1 + s2))
    def _():
        for kk in range(kb):
            k1_idx = (s - s1) * kb + kk
            gmat = gs_ref[k1_idx] if gs_resident else gs_ref[kk]
            for c0 in range(0, cb, INNER_LANES):
                u = _inner_forward(scr_ref, k1_idx, c0, gmat)
                ur, ui = u[:n2], u[n2:]
                kr = kf_ref[kk, 0, :, c0:c0 + INNER_LANES].astype(F32)
                ki = kf_ref[kk, 1, :, c0:c0 + INNER_LANES].astype(F32)
                v = jnp.concatenate([ur * kr - ui * ki, ur * ki + ui * kr], axis=0).astype(BF16)
                b = jax.lax.dot_general(gmat, v, (((0,), (0,)), ((), ())),
                                        preferred_element_type=F32)
                scr_ref[k1_idx, :, :, c0:c0 + INNER_LANES] = (
                    b.reshape(2, n2, INNER_LANES).astype(BF16))

    @pl.when(s >= s1 + s2)
    def _():
        for p0 in range(0, GROUP, PACK):
            row0 = pl.multiple_of((s - s1 - s2) * GROUP + p0, PACK)
            b = scr_ref[:, :, pl.ds(row0, PACK), :].astype(F32)
            lo, hi = [jnp.dot(k3_ref[...], x, preferred_element_type=F32) for x in _halves(b)]
            y = _join(lo, hi, (2, q))
            rows = slice(p0, p0 + PACK)
            z = g_ref[:, :, rows, :].astype(F32) * (y + u_ref[:, :, rows, :].astype(F32) * d_ref[...])
            o_ref[:, :, rows, :] = z.astype(o_ref.dtype)


def _long_conv(u6, u_off, g6, g_off, d, kf, order, dc, cb, kb):
    bp, _, q, n2p, _, _ = u6.shape
    C = D_HYENA
    n1, n2 = dc["n1"], dc["n2"]
    s1, s2 = n2p, n1 // kb
    koff = order * (C // cb)
    grp1 = lambda s: jnp.where(s < s1, s, jnp.where(s < s1 + s2, s1 - 1, s - s1 - s2))
    grp3 = lambda s: jnp.maximum(s - s1 - s2, 0)
    kblk = lambda s: jnp.clip(s - s1, 0, s2 - 1)
    tspec = lambda off, grp: pl.BlockSpec((None, 2, q, None, GROUP, cb),
                                          lambda b, c, s: (b, 0, 0, grp(s), 0, off + c))
    const = lambda a: pl.BlockSpec(a.shape, lambda b, c, s: (0,) * a.ndim,
                                   pipeline_mode=pl.Buffered(1))
    return pl.pallas_call(
        functools.partial(_conv_kernel, s1=s1, s2=s2),
        grid=(bp, C // cb, s1 + s2 + s1),
        in_specs=[tspec(u_off, grp1), tspec(g_off, grp3),
                  pl.BlockSpec((1, cb), lambda b, c, s: (0, c)),
                  const(dc["k1"]), const(dc["k3"]),
                  (const(dc["gs"]) if dc["gs"].size * 2 <= GS_RESIDENT_BYTES else
                   pl.BlockSpec((kb, 2 * n2, 2 * n2), lambda b, c, s: (kblk(s), 0, 0))),
                  pl.BlockSpec((kb, 2, n2, cb), lambda b, c, s: (kblk(s), 0, 0, koff + c))],
        out_specs=tspec(0, grp3),
        out_shape=jax.ShapeDtypeStruct((bp, 2, q, n2p, GROUP, C), BF16),
        scratch_shapes=[pltpu.VMEM((n1, 2, n2, cb), BF16)],
        compiler_params=_params("parallel", "parallel", "arbitrary"),
        name="long_conv",
    )(u6, g6, d.reshape(1, C), dc["k1"], dc["k3"], dc["gs"], kf)


def _hyena(hyc, kf, dc, hy_d, cb, kb):
    B, L, _ = hyc.shape
    C = D_HYENA
    n2, q = dc["n2"], dc["q"]
    bp = B * L // (2 * q * n2)
    hy6 = hyc.reshape(bp, 2, q, n2 // GROUP, GROUP, 3 * C)
    z6 = _long_conv(hy6, 0, hy6, C // cb, hy_d[0], kf, 0, dc, cb, kb)
    z6 = _long_conv(z6, 0, hy6, 2 * (C // cb), hy_d[1], kf, 1, dc, cb, kb)
    return z6.reshape(B, L, C)


def _filt_kernel(xf_ref, xb_ref, nrm_ref, k1f_ref, k1b_ref, gs_ref, o_ref, scr_ref, *, s1):
    n1, _, n2, cb = scr_ref.shape
    kb = gs_ref.shape[0]
    s = pl.program_id(1)

    @pl.when(s < s1)
    def _():
        d = lambda k, x: jnp.dot(k[...], x, preferred_element_type=F32)
        for p0 in range(0, GROUP, PACK):
            f_lo, f_hi = _halves(xf_ref[:, p0:p0 + PACK, :].astype(F32))
            b_lo, b_hi = _halves(xb_ref[:, GROUP - PACK - p0:GROUP - p0, :].astype(F32))
            lo = d(k1f_ref, f_lo) + d(k1b_ref, b_hi)
            hi = d(k1f_ref, f_hi) + d(k1b_ref, b_lo)
            row0 = pl.multiple_of(s * GROUP + p0, PACK)
            scr_ref[:, :, pl.ds(row0, PACK), :] = _join(lo, hi, (n1, 2)).astype(BF16)

    @pl.when(s >= s1)
    def _():
        for kk in range(kb):
            for c0 in range(0, cb, INNER_LANES):
                inv = 1.0 / nrm_ref[:, c0:c0 + INNER_LANES]
                u = _inner_forward(scr_ref, (s - s1) * kb + kk, c0, gs_ref[kk]) * inv
                o_ref[kk, :, :, c0:c0 + INNER_LANES] = (
                    u.reshape(2, n2, INNER_LANES).astype(o_ref.dtype))


def _filter_spectrum(L, dc, fw1, fb1, ff1, fw2, fb2, ff2, fw3, fdecay, cb, kb):
    hf, hb, nrm = _filter_mlp(L, fw1, fb1, ff1, fw2, fb2, ff2, fw3, fdecay)
    nf = hf.shape[1]
    n1, n2 = dc["n1"], dc["n2"]
    n1h, n2p = n1 // 2, n2 // GROUP
    s1, s2 = n2p, n1 // kb
    hf4 = hf.reshape(n1h, n2p, GROUP, nf)
    hb4 = hb.reshape(n1h, n2p, GROUP, nf)
    grp = lambda s: jnp.minimum(s, s1 - 1)
    kblk = lambda s: jnp.maximum(s - s1, 0)
    const = lambda a: pl.BlockSpec(a.shape, lambda c, s: (0,) * a.ndim,
                                   pipeline_mode=pl.Buffered(1))
    return pl.pallas_call(
        functools.partial(_filt_kernel, s1=s1),
        grid=(nf // cb, s1 + s2),
        in_specs=[pl.BlockSpec((n1h, None, GROUP, cb), lambda c, s: (0, grp(s), 0, c)),
                  pl.BlockSpec((n1h, None, GROUP, cb), lambda c, s: (0, s1 - 1 - grp(s), 0, c)),
                  pl.BlockSpec((1, cb), lambda c, s: (0, c)),
                  const(dc["k1f"]), const(dc["k1b"]),
                  pl.BlockSpec((kb, 2 * n2, 2 * n2), lambda c, s: (kblk(s), 0, 0))],
        out_specs=pl.BlockSpec((kb, 2, n2, cb), lambda c, s: (kblk(s), 0, 0, c)),
        out_shape=jax.ShapeDtypeStruct((n1, 2, n2, nf), BF16),
        scratch_shapes=[pltpu.VMEM((n1, 2, n2, cb), BF16)],
        compiler_params=_params("parallel", "arbitrary"),
        name="filter_spectrum",
    )(hf4, hb4, nrm, dc["k1f"], dc["k1b"], dc["gs"])


OUT_CHAINS = 2


def _out_kernel(a_ref, z_ref, gh_ref, hg_ref, w_ref, h_ref, g_ref, b_ref, o_ref,
                *maybe_ob_ref):
    chain = a_ref.shape[0] // OUT_CHAINS
    for r0 in range(0, a_ref.shape[0], chain):
        rows = slice(r0, r0 + chain)
        z = z_ref[rows, :].astype(F32)
        gh = gh_ref[rows, :].astype(F32)
        zn = z * jax.lax.rsqrt(jnp.mean(z * z, axis=-1, keepdims=True) + NORM_EPS) * hg_ref[...]
        zn = (zn * (gh * jax.nn.sigmoid(gh))).astype(BF16)
        acc = jnp.dot(jnp.concatenate([a_ref[rows, :], zn], axis=1), w_ref[...],
                      preferred_element_type=F32)
        y = DN_ALPHA * h_ref[rows, :] + acc
        mu = jnp.mean(y, axis=-1, keepdims=True)
        yc = y - mu
        var = jnp.mean(yc * yc, axis=-1, keepdims=True)
        out = yc * jax.lax.rsqrt(var + NORM_EPS) * g_ref[...] + b_ref[...]
        o_ref[rows, :] = out
        for ob_ref in maybe_ob_ref:
            ob_ref[rows, :] = out.astype(BF16)


def _out_proj(a, z, proj, hy_g, w_out, h, ln_g, ln_b, want_bf16, tm=512):
    T, D = h.shape
    n_out = 2 if want_bf16 else 1
    row = lambda w: pl.BlockSpec((tm, w), lambda i: (i, 0))
    const = lambda r, w: pl.BlockSpec((r, w), lambda i: (0, 0), pipeline_mode=pl.Buffered(1))
    return pl.pallas_call(
        _out_kernel,
        grid=(T // tm,),
        in_specs=[row(D_ATTN), row(D_HYENA),
                  pl.BlockSpec((tm, D_HYENA), lambda i: (i, COL_GH)),
                  const(1, D_HYENA), const(D_ATTN + D_HYENA, D), row(D),
                  const(1, D), const(1, D)],
        out_specs=[row(D), row(D)][:n_out],
        out_shape=[jax.ShapeDtypeStruct((T, D), F32), jax.ShapeDtypeStruct((T, D), BF16)][:n_out],
        compiler_params=_params("parallel"),
        name="out_proj",
    )(a, z, proj, hy_g.reshape(1, -1), w_out.astype(BF16), h, ln_g.reshape(1, D),
      ln_b.reshape(1, D))


def _pair_lane_order(w):
    rows = w.shape[0]
    w5 = w.reshape(rows, -1, 2, 8, 8)
    octet_of_slot = np.argsort(_SLOT_OF_OCTET)
    w5 = jnp.concatenate([w5[:, :, :, o:o + 1, :] for o in octet_of_slot], axis=3)
    return w5.transpose(0, 1, 3, 2, 4).reshape(rows, -1)


def _arrange_w_in(w):
    q, k, v, ga, hy, gh = jnp.split(w, [1024, 1280, 1536, 2560, 5632], axis=1)
    plain = jnp.concatenate([_pair_lane_order(q), ga, gh, _pair_lane_order(k), v], axis=1)
    return plain.astype(BF16), hy.astype(BF16)


def _trunk(x, p):
    B, L, D = x.shape
    T = B * L
    pairs = B > 1
    dc = _dft_constants(L, pairs)
    cb, kb, kb_filt = (512, 32, 32) if L <= 2048 else (256, 16, 16)
    rope_tab = _rope_table(L)
    bias = _attn_bias()
    h, hb = _layernorm(x.reshape(T, D), p["emb_ln_g"], p["emb_ln_b"])
    for l in range(DEPTH):
        w_plain, w_hy = _arrange_w_in(p["w_in"][l])
        proj = _matmul(hb, w_plain, 1024, D_PLAIN // 2, BF16)
        proj3 = proj.reshape(B, L, D_PLAIN)
        a = _attention(proj3, rope_tab, bias, p["attn_sink"][l], p["attn_norm_g"][l])
        hyc = _proj_conv(hb, w_hy, p["conv_w"][l], p["conv_b"][l], L).reshape(B, L, 3 * D_HYENA)
        kf = _filter_spectrum(L, dc, p["flt_w1"][l], p["flt_b1"][l], p["flt_freq1"][l],
                              p["flt_w2"][l], p["flt_b2"][l], p["flt_freq2"][l],
                              p["flt_w3"][l], p["flt_decay"][l], cb, kb_filt)
        z = _hyena(hyc, kf, dc, p["hyena_d"][l], cb, kb)
        outs = _out_proj(a.reshape(T, D_ATTN), z.reshape(T, D_HYENA), proj,
                         p["hyena_norm_g"][l], p["w_out"][l], h, p["ln_g"][l], p["ln_b"][l],
                         want_bf16=l + 1 < DEPTH)
        h, hb = outs[0], outs[-1]
    return h.reshape(B, L, D)


def kernel(x_prompt, x_sample, emb_ln_g, emb_ln_b, w_in, attn_sink, conv_w, conv_b, flt_w1,
           flt_b1, flt_freq1, flt_w2, flt_b2, flt_freq2, flt_w3, flt_decay, hyena_d,
           attn_norm_g, hyena_norm_g, w_out, ln_g, ln_b):
    p = dict(emb_ln_g=emb_ln_g, emb_ln_b=emb_ln_b, w_in=w_in, attn_sink=attn_sink,
             conv_w=conv_w, conv_b=conv_b, flt_w1=flt_w1, flt_b1=flt_b1, flt_freq1=flt_freq1,
             flt_w2=flt_w2, flt_b2=flt_b2, flt_freq2=flt_freq2, flt_w3=flt_w3,
             flt_decay=flt_decay, hyena_d=hyena_d, attn_norm_g=attn_norm_g,
             hyena_norm_g=hyena_norm_g, w_out=w_out, ln_g=ln_g, ln_b=ln_b)
    return (_trunk(x_prompt, p), _trunk(x_sample, p))
```
